```python
import jax
import jax.numpy as jnp
from jax import lax
import numpy as np

D_MODEL = 4096
BATCH = 1
SEQ = 16384
DEPTH = 4

CTX_LEN = 256
GRID_W = 64
N_MIXERS = 2
N_MLSTM_LAYERS = (DEPTH + 1) // 2
N_CONV_LAYERS = DEPTH // 2
M_HEADS = 8
M_DQK = D_MODEL // (2 * M_HEADS)
M_DV = D_MODEL // M_HEADS
M_CHUNK = 128
FORGET_BIAS = 3.0
CONV_W = 31
N_EXPERTS = 64
TOP_K = 6
EXPERT_DIM = 192
SHARED_DIM = 192
ROUTED_SCALE = 2.5
ADA_RANK = 256
N_MOD = 6
EPS = 1e-6

kernel_name = 'hybrid_mlstm_conformer_moe_dit'


def _rmsnorm(h, g):
    h32 = h.astype(jnp.float32)
    y = h32 * lax.rsqrt(jnp.mean(h32 * h32, axis=-1, keepdims=True) + EPS)
    return (y * g.astype(jnp.float32)).astype(h.dtype)


def _ada(cond, down, up, bias):
    m = (jax.nn.silu(cond) @ down) @ up + bias
    return jnp.split(m[..., None, :], N_MOD, axis=-1)


def _modulate(h, g, shift, scale):
    return _rmsnorm(h, g) * (1 + scale) + shift


def _to_chunks(a):
    b, t = a.shape[:2]
    a = a.reshape((b, t // M_CHUNK, M_CHUNK) + a.shape[2:])
    return jnp.moveaxis(a, (1, 2), (0, 3))


def _from_chunks(a):
    a = jnp.moveaxis(a, (0, 3), (1, 2))
    return a.reshape((a.shape[0], a.shape[1] * a.shape[2]) + a.shape[3:])


def _mlstm_scan(q, k, v, ig, lf, state):
    tri = jnp.tril(jnp.ones((M_CHUNK, M_CHUNK), dtype=bool))

    def step(carry, chunk):
        C, n, m = carry
        qc, kc, vc, ic, fc = chunk
        b = jnp.cumsum(fc, axis=-1)
        dmat = jnp.where(tri, b[..., :, None] - b[..., None, :] + ic[..., None, :], -jnp.inf)
        m_inter = b + m[..., None]
        m_t = jnp.maximum(m_inter, jnp.max(dmat, axis=-1))
        w_intra = jnp.exp(dmat - m_t[..., None])
        w_inter = jnp.exp(m_inter - m_t)
        s = jnp.einsum('bhtd,bhsd->bhts', qc, kc) * w_intra
        num = jnp.einsum('bhts,bhsv->bhtv', s, vc) + w_inter[..., None] * jnp.einsum('bhtd,bhdv->bhtv', qc, C)
        den = jnp.sum(s, axis=-1) + w_inter * jnp.einsum('bhtd,bhd->bht', qc, n)
        h = num / jnp.maximum(jnp.abs(den), jnp.exp(-m_t))[..., None]
        b_last = b[..., -1]
        log_w = b_last[..., None] - b + ic
        m_new = jnp.maximum(b_last + m, jnp.max(log_w, axis=-1))
        kw = kc * jnp.exp(log_w - m_new[..., None])[..., None]
        decay = jnp.exp(b_last + m - m_new)
        C = decay[..., None, None] * C + jnp.einsum('bhsd,bhsv->bhdv', kw, vc)
        n = decay[..., None] * n + jnp.sum(kw, axis=-2)
        return (C, n, m_new), h

    xs = (_to_chunks(q), _to_chunks(k), _to_chunks(v), _to_chunks(ig), _to_chunks(lf))
    state, h = lax.scan(step, state, xs)
    return _from_chunks(h), state


def _mlstm_mixer(ux, uc, w_in, gate_bias, head_norm, w_out, ctx_out):
    H, DK, DV = M_HEADS, M_DQK, M_DV
    f32 = jnp.float32
    splits = np.cumsum([H * DK, H * DK, H * DV, H * DV]).tolist()

    def project(u):
        bsz, t = u.shape[:2]
        q, k, v, o, g = jnp.split(u @ w_in, splits, axis=-1)
        q = q.reshape(bsz, t, H, DK).astype(f32) * DK ** -0.5
        k = k.reshape(bsz, t, H, DK).astype(f32)
        v = v.reshape(bsz, t, H, DV).astype(f32)
        g = (g.astype(f32) + gate_bias.astype(f32)).reshape(bsz, t, 4, H)
        fwd = (g[:, :, 0], jax.nn.log_sigmoid(g[:, :, 1]))
        bwd = (g[:, :, 2], jax.nn.log_sigmoid(g[:, :, 3]))
        return q, k, v, o, fwd, bwd

    def flip(a):
        return jnp.flip(a, axis=1)

    def zero_state(bsz):
        return (jnp.zeros((bsz, H, DK, DV), f32), jnp.zeros((bsz, H, DK), f32), jnp.zeros((bsz, H), f32))

    def readout(h, o, dtype):
        bsz, t = h.shape[:2]
        h = h * lax.rsqrt(jnp.mean(h * h, axis=-1, keepdims=True) + EPS)
        h = h.reshape(bsz, t, H * DV) * head_norm.astype(f32) * jax.nn.sigmoid(o.astype(f32))
        return h.astype(dtype) @ w_out

    bsz = ux.shape[0]
    qx, kx, vx, ox, gxf, gxb = project(ux)
    qc, kc, vc, oc, gcf, gcb = project(uc)
    hcf, st_f = _mlstm_scan(qc, kc, vc, gcf[0], gcf[1], zero_state(bsz))
    hxf, _ = _mlstm_scan(qx, kx, vx, gxf[0], gxf[1], st_f)
    hcb, st_b = _mlstm_scan(flip(qc), flip(kc), flip(vc), flip(gcb[0]), flip(gcb[1]), zero_state(bsz))
    hxb, _ = _mlstm_scan(flip(qx), flip(kx), flip(vx), flip(gxb[0]), flip(gxb[1]), st_b)
    yx = readout(hxf + flip(hxb), ox, ux.dtype)
    yc = readout(hcf + flip(hcb), oc, uc.dtype) if ctx_out else None
    return yx, yc


def _conv_mixer(u, w_in, w_dw, ln_g, ln_b, w_out, n_seq):
    bsz, t, d = u.shape
    a, gate = jnp.split(u @ w_in, 2, axis=-1)
    h = (a * jax.nn.sigmoid(gate)).reshape(bsz * n_seq, t // n_seq, d)
    h = lax.conv_general_dilated(h, w_dw[:, None, :].astype(h.dtype), window_strides=(1,),
                                 padding=[(CONV_W // 2, CONV_W // 2)],
                                 dimension_numbers=('NWC', 'WIO', 'NWC'), feature_group_count=d)
    h = h.reshape(bsz, t, d).astype(jnp.float32)
    mu = jnp.mean(h, axis=-1, keepdims=True)
    var = jnp.mean(jnp.square(h - mu), axis=-1, keepdims=True)
    h = (h - mu) * lax.rsqrt(var + EPS) * ln_g.astype(jnp.float32) + ln_b.astype(jnp.float32)
    return jax.nn.silu(h).astype(u.dtype) @ w_out


def _moe(u, w_router, router_bias, e_w_gate, e_w_up, e_w_down, s_w_gate, s_w_up, s_w_down):
    f32 = jnp.float32
    scores = jax.nn.sigmoid((u @ w_router).astype(f32))
    _, idx = lax.top_k(scores + router_bias.astype(f32), TOP_K)
    w = jnp.take_along_axis(scores, idx, axis=-1)
    w = w / jnp.sum(w, axis=-1, keepdims=True) * ROUTED_SCALE
    gates = jnp.sum(jax.nn.one_hot(idx, N_EXPERTS, dtype=f32) * w[..., None], axis=-2).astype(u.dtype)
    h = jax.nn.silu(jnp.einsum('btd,edi->btei', u, e_w_gate)) * jnp.einsum('btd,edi->btei', u, e_w_up)
    routed = jnp.einsum('btei,eid->btd', h * gates[..., None], e_w_down)
    shared = (jax.nn.silu(u @ s_w_gate) * (u @ s_w_up)) @ s_w_down
    return routed + shared


def setup_inputs(seed: int = 0) -> dict:
    key = jax.random.key(seed)
    keys = iter(jax.random.split(key, 32))

    def nrm(shape, scale):
        return jax.random.normal(next(keys), shape, jnp.float32) * scale

    D, L, NA, NB = D_MODEL, DEPTH, N_MLSTM_LAYERS, N_CONV_LAYERS
    H, E = M_HEADS, N_EXPERTS
    p_a = 2 * H * M_DQK + 2 * H * M_DV + 4 * H
    gate_offset = jnp.array([0.0, FORGET_BIAS, 0.0, FORGET_BIAS], jnp.float32)[None, :, None]
    return {
        'x': nrm((BATCH, SEQ, D), 1.0),
        'c': nrm((BATCH, D), 1.0),
        'ctx': nrm((BATCH, CTX_LEN, D), 1.0),
        'c_ctx': nrm((D,), 1.0),
        'ada_down': nrm((L, D, ADA_RANK), D ** -0.5),
        'ada_up': nrm((L, ADA_RANK, N_MOD * D), 0.2 * ADA_RANK ** -0.5),
        'ada_bias': nrm((L, N_MOD * D), 0.02),
        'norm_mix': 1.0 + nrm((L, D), 0.02),
        'norm_ffn': 1.0 + nrm((L, D), 0.02),
        'a_w_in': nrm((NA, D, p_a), D ** -0.5),
        'a_gate_bias': (nrm((NA, 4, H), 0.1) + gate_offset).reshape(NA, 4 * H),
        'a_head_norm': 1.0 + nrm((NA, H * M_DV), 0.02),
        'a_w_out': nrm((NA, H * M_DV, D), (H * M_DV) ** -0.5),
        'b_w_in': nrm((NB, D, 2 * D), D ** -0.5),
        'b_w_dw': nrm((NB, CONV_W, D), CONV_W ** -0.5),
        'b_ln_g': 1.0 + nrm((NB, D), 0.02),
        'b_ln_b': nrm((NB, D), 0.02),
        'b_w_out': nrm((NB, D, D), D ** -0.5),
        'w_router': nrm((L, D, E), D ** -0.5),
        'router_bias': nrm((L, E), 0.01),
        'e_w_gate': nrm((L, E, D, EXPERT_DIM), D ** -0.5),
        'e_w_up': nrm((L, E, D, EXPERT_DIM), D ** -0.5),
        'e_w_down': nrm((L, E, EXPERT_DIM, D), EXPERT_DIM ** -0.5),
        's_w_gate': nrm((L, D, SHARED_DIM), D ** -0.5),
        's_w_up': nrm((L, D, SHARED_DIM), D ** -0.5),
        's_w_down': nrm((L, SHARED_DIM, D), SHARED_DIM ** -0.5),
        'final_norm': 1.0 + nrm((D,), 0.02),
    }


def reference(x, c, ctx, c_ctx, ada_down, ada_up, ada_bias, norm_mix, norm_ffn,
              a_w_in, a_gate_bias, a_head_norm, a_w_out,
              b_w_in, b_w_dw, b_ln_g, b_ln_b, b_w_out,
              w_router, router_bias, e_w_gate, e_w_up, e_w_down,
              s_w_gate, s_w_up, s_w_down, final_norm):
    rows = x.shape[1] // GRID_W
    h_ctx = ctx
    for i in range(DEPTH):
        kind = i % N_MIXERS
        j = i // N_MIXERS
        ctx_live = any(l % N_MIXERS == 0 for l in range(i + 1, DEPTH))
        need_ctx = kind == 0 or ctx_live
        sh1, sc1, g1, sh2, sc2, g2 = _ada(c, ada_down[i], ada_up[i], ada_bias[i])
        ux = _modulate(x, norm_mix[i], sh1, sc1)
        if need_ctx:
            csh1, csc1, cg1, csh2, csc2, cg2 = _ada(c_ctx, ada_down[i], ada_up[i], ada_bias[i])
            uc = _modulate(h_ctx, norm_mix[i], csh1, csc1)
        if kind == 0:
            yx, yc = _mlstm_mixer(ux, uc, a_w_in[j], a_gate_bias[j], a_head_norm[j], a_w_out[j], ctx_live)
        else:
            yx = _conv_mixer(ux, b_w_in[j], b_w_dw[j], b_ln_g[j], b_ln_b[j], b_w_out[j], rows)
            yc = _conv_mixer(uc, b_w_in[j], b_w_dw[j], b_ln_g[j], b_ln_b[j], b_w_out[j], 1) if ctx_live else None
        x = x + g1 * yx
        x = x + g2 * _moe(_modulate(x, norm_ffn[i], sh2, sc2), w_router[i], router_bias[i],
                          e_w_gate[i], e_w_up[i], e_w_down[i], s_w_gate[i], s_w_up[i], s_w_down[i])
        if ctx_live:
            h_ctx = h_ctx + cg1 * yc
            h_ctx = h_ctx + cg2 * _moe(_modulate(h_ctx, norm_ffn[i], csh2, csc2), w_router[i], router_bias[i],
                                       e_w_gate[i], e_w_up[i], e_w_down[i], s_w_gate[i], s_w_up[i], s_w_down[i])
    return _rmsnorm(x, final_norm)
```

```python
import functools

import jax
import jax.numpy as jnp
from jax import lax
from jax.experimental import pallas as pl
from jax.experimental.pallas import tpu as pltpu

F32 = jnp.float32
BF16 = jnp.bfloat16

GRID_W = 64
M_HEADS = 8
M_CHUNK = 128
CONV_W = 31
CONV_HALF = CONV_W // 2
TOP_K = 6
ROUTED_SCALE = 2.5
N_MOD = 6
EPS = 1e-6

LANES = 128
SUBLANES = 8
VMEM_LIMIT = 56 * 1024 * 1024
HIGHEST = lax.Precision.HIGHEST


def _cparams(sem):
    return pltpu.CompilerParams(dimension_semantics=sem, vmem_limit_bytes=VMEM_LIMIT)


def _sigmoid(v):
    return 1.0 / (1.0 + jnp.exp(-v))


def _silu(v):
    return v * _sigmoid(v)


def _log_sigmoid(v):
    return jnp.minimum(v, 0.0) - jnp.log(1.0 + jnp.exp(-jnp.abs(v)))


def _ada_body(cond_ref, down_ref, up_ref, bias_ref, o_ref):
    s = _silu(cond_ref[...])
    t = jnp.dot(s, down_ref[...], precision=HIGHEST, preferred_element_type=F32)
    o_ref[...] = jnp.dot(t, up_ref[...], precision=HIGHEST, preferred_element_type=F32) + bias_ref[...]


def _ada_all(cond, ada_down, ada_up, ada_bias):
    n_layers, d, rank = ada_down.shape
    n_out = ada_up.shape[-1]
    tn = d
    assert n_out % tn == 0
    return pl.pallas_call(
        _ada_body,
        out_shape=jax.ShapeDtypeStruct((n_layers, SUBLANES, n_out), F32),
        grid=(n_layers, n_out // tn),
        in_specs=[
            pl.BlockSpec((SUBLANES, d), lambda l, j: (0, 0)),
            pl.BlockSpec((None, d, rank), lambda l, j: (l, 0, 0)),
            pl.BlockSpec((None, rank, tn), lambda l, j: (l, 0, j)),
            pl.BlockSpec((None, 1, tn), lambda l, j: (l, 0, j)),
        ],
        out_specs=pl.BlockSpec((None, SUBLANES, tn), lambda l, j: (l, 0, j)),
        compiler_params=_cparams(("parallel", "arbitrary")),
        name="ada_mod",
    )(cond, ada_down, ada_up, ada_bias.reshape(n_layers, 1, n_out))


_PRO_ROWS = 32


def _mm_body(*refs, prologue, epilogue, tm):
    refs = list(refs)
    a_ref = refs.pop(0)
    if prologue == "mod":
        g_ref, sc_ref, sh_ref = refs.pop(0), refs.pop(0), refs.pop(0)
    elif prologue == "ln_silu":
        g_ref, b_ref = refs.pop(0), refs.pop(0)
    w_ref = refs.pop(0)
    if epilogue == "glu":
        w2_ref = refs.pop(0)
    elif epilogue == "residual":
        res_ref, gate_ref = refs.pop(0), refs.pop(0)
    o_ref = refs.pop(0)

    if prologue == "none":
        u = a_ref[...]
    else:
        u_ref = refs.pop(0)

        @pl.when(pl.program_id(1) == 0)
        def _():
            def rows(r, carry):
                sl = pl.ds(pl.multiple_of(r * _PRO_ROWS, _PRO_ROWS), _PRO_ROWS)
                x = a_ref[sl, :]
                if prologue == "mod":
                    y = x * lax.rsqrt(jnp.mean(x * x, axis=-1, keepdims=True) + EPS) * g_ref[...]
                    y = y * (1.0 + sc_ref[...]) + sh_ref[...]
                else:
                    mu = jnp.mean(x, axis=-1, keepdims=True)
                    xc = x - mu
                    var = jnp.mean(xc * xc, axis=-1, keepdims=True)
                    y = _silu(xc * lax.rsqrt(var + EPS) * g_ref[...] + b_ref[...])
                u_ref[sl, :] = y.astype(BF16)
                return carry

            lax.fori_loop(0, tm // _PRO_ROWS, rows, 0)

        u = u_ref[...]

    acc = jnp.dot(u, w_ref[...].astype(BF16), preferred_element_type=F32)
    if epilogue == "glu":
        acc2 = jnp.dot(u, w2_ref[...].astype(BF16), preferred_element_type=F32)
        acc = acc * _sigmoid(acc2)
    elif epilogue == "residual":
        acc = res_ref[...] + gate_ref[...] * acc
    o_ref[...] = acc.astype(o_ref.dtype)


def _fused_matmul(a, w, *, n_out, prologue="none", pro_args=(), epilogue="plain",
                  epi_args=(), w2_col_offset=0, out_dtype=F32, tm=512, tn=512, name="mm"):
    m, k = a.shape
    tm = min(tm, m)
    tn = min(tn, n_out)
    assert m % tm == 0 and n_out % tn == 0 and w.shape[0] == k
    vec = lambda n: pl.BlockSpec((1, n), lambda i, j: (0, 0))
    in_specs = [pl.BlockSpec((tm, k), lambda i, j: (i, 0))]
    args = [a]
    in_specs += [vec(k)] * len(pro_args)
    args += list(pro_args)
    in_specs.append(pl.BlockSpec((k, tn), lambda i, j: (0, j)))
    args.append(w)
    if epilogue == "glu":
        off = w2_col_offset // tn
        in_specs.append(pl.BlockSpec((k, tn), lambda i, j: (0, j + off)))
        args.append(w)
    elif epilogue == "residual":
        res, gate = epi_args
        in_specs += [pl.BlockSpec((tm, tn), lambda i, j: (i, j)),
                     pl.BlockSpec((1, tn), lambda i, j: (0, j))]
        args += [res, gate]
    scratch = [] if prologue == "none" else [pltpu.VMEM((tm, k), BF16)]
    return pl.pallas_call(
        functools.partial(_mm_body, prologue=prologue, epilogue=epilogue, tm=tm),
        out_shape=jax.ShapeDtypeStruct((m, n_out), out_dtype),
        grid=(m // tm, n_out // tn),
        in_specs=in_specs,
        out_specs=pl.BlockSpec((tm, tn), lambda i, j: (i, j)),
        scratch_shapes=scratch,
        compiler_params=_cparams(("parallel", "arbitrary")),
        name=name,
    )(*args)


def _scan_body(*refs, reverse, finalize, dk, dv):
    refs = list(refs)
    q_ref, k_ref, v_ref = refs.pop(0), refs.pop(0), refs.pop(0)
    gcol_ref, grow_ref, bias_row_ref, bias_col_ref = refs.pop(0), refs.pop(0), refs.pop(0), refs.pop(0)
    c0_ref, n0_ref, m0_ref = refs.pop(0), refs.pop(0), refs.pop(0)
    if finalize:
        o_ref, hprev_ref, hnorm_ref = refs.pop(0), refs.pop(0), refs.pop(0)
    out_ref, c_ref, n_ref, m_ref = refs.pop(0), refs.pop(0), refs.pop(0), refs.pop(0)

    L = M_CHUNK
    H = M_HEADS

    @pl.when(pl.program_id(0) == 0)
    def _():
        c_ref[...] = c0_ref[...]
        n_ref[...] = n0_ref[...]
        m_ref[...] = m0_ref[...]

    kind_i = 2 if reverse else 0
    kind_f = kind_i + 1

    t_idx = lax.broadcasted_iota(jnp.int32, (L, L), 0)
    s_idx = lax.broadcasted_iota(jnp.int32, (L, L), 1)
    mask = (s_idx >= t_idx) if reverse else (s_idx <= t_idx)
    mask_f = mask.astype(F32)
    mask_t = ((t_idx >= s_idx) if reverse else (t_idx <= s_idx)).astype(F32)
    q_scale = dk ** -0.5

    gcol = gcol_ref[...] + bias_row_ref[...]
    grow = grow_ref[...] + bias_col_ref[...]
    lf_col = _log_sigmoid(gcol)
    lf_row = _log_sigmoid(grow[kind_f * H:(kind_f + 1) * H, :])
    b_col_all = jnp.dot(mask_f, lf_col, precision=HIGHEST, preferred_element_type=F32)
    b_row_all = jnp.dot(lf_row, mask_t, precision=HIGHEST, preferred_element_type=F32)

    for h in range(H):
        ci, cf = kind_i * H + h, kind_f * H + h
        b_col = b_col_all[:, cf:cf + 1]
        b_row = b_row_all[h:h + 1, :]
        i_col = gcol[:, ci:ci + 1]
        i_row = grow[ci:ci + 1, :]
        m_prev = m_ref[h][:, :1]
        q = q_ref[:, h * dk:(h + 1) * dk]
        k = k_ref[:, h * dk:(h + 1) * dk]
        v = v_ref[:, h * dv:(h + 1) * dv]
        c_state = c_ref[h]
        n_state = n_ref[h]

        dmat = jnp.where(mask, b_col - b_row + i_row, -jnp.inf)
        m_inter = b_col + m_prev
        m_t = jnp.maximum(m_inter, jnp.max(dmat, axis=-1, keepdims=True))
        w_intra = jnp.exp(dmat - m_t) * q_scale
        w_inter = jnp.exp(m_inter - m_t) * q_scale
        s = lax.dot_general(q, k, (((1,), (1,)), ((), ())), preferred_element_type=F32) * w_intra
        num = jnp.dot(s.astype(BF16), v, preferred_element_type=F32)
        num = num + w_inter * jnp.dot(q, c_state.astype(BF16), preferred_element_type=F32)
        qn = jnp.sum(q.astype(F32) * n_state, axis=-1, keepdims=True)
        den = jnp.sum(s, axis=-1, keepdims=True) + w_inter * qn
        hv = num * (1.0 / jnp.maximum(jnp.abs(den), jnp.exp(-m_t)))

        b_last = jnp.sum(lf_row[h:h + 1, :], axis=-1, keepdims=True)
        m_new = jnp.maximum(b_last + m_prev, jnp.max(b_last - b_row + i_row, axis=-1, keepdims=True))
        kw = k.astype(F32) * jnp.exp(b_last - b_col + i_col - m_new)
        decay = jnp.exp(b_last + m_prev - m_new)
        c_ref[h] = decay * c_state + lax.dot_general(
            kw.astype(BF16), v, (((0,), (0,)), ((), ())), preferred_element_type=F32)
        n_ref[h] = decay * n_state + jnp.sum(kw, axis=0, keepdims=True)
        m_ref[h] = jnp.broadcast_to(m_new, (1, LANES))

        cols = slice(h * dv, (h + 1) * dv)
        if finalize:
            ht = hv + hprev_ref[:, cols]
            ht = ht * lax.rsqrt(jnp.mean(ht * ht, axis=-1, keepdims=True) + EPS)
            ht = ht * hnorm_ref[:, cols] * _sigmoid(o_ref[:, cols].astype(F32))
            out_ref[:, cols] = ht.astype(out_ref.dtype)
        else:
            out_ref[:, cols] = hv


def _mlstm_scan(proj, gcol, grow, bias_row, bias_col, state, *, reverse, finalize=None):
    t = proj.shape[0]
    H, L = M_HEADS, M_CHUNK
    c0, n0, m0 = state
    dk, dv = c0.shape[1], c0.shape[2]
    nc = t // L
    cidx = (lambda i: nc - 1 - i) if reverse else (lambda i: i)
    qk_w, v_w = H * dk, H * dv
    assert 2 * qk_w == v_w
    full = lambda shape: pl.BlockSpec(shape, lambda i: (0,) * len(shape))
    in_specs = [
        pl.BlockSpec((L, qk_w), lambda i: (cidx(i), 0)),
        pl.BlockSpec((L, qk_w), lambda i: (cidx(i), 1)),
        pl.BlockSpec((L, v_w), lambda i: (cidx(i), 1)),
        pl.BlockSpec((L, LANES), lambda i: (cidx(i), 0)),
        pl.BlockSpec((LANES, L), lambda i: (0, cidx(i))),
        full((1, LANES)), full((LANES, 1)),
        full(c0.shape), full(n0.shape), full(m0.shape),
    ]
    args = [proj, proj, proj, gcol, grow, bias_row, bias_col, c0, n0, m0]
    if finalize is not None:
        hprev, hnorm = finalize
        in_specs += [pl.BlockSpec((L, v_w), lambda i: (cidx(i), 2)),
                     pl.BlockSpec((L, v_w), lambda i: (cidx(i), 0)),
                     full((1, v_w))]
        args += [proj, hprev, hnorm]
    out_dtype = BF16 if finalize is not None else F32
    out, c1, n1, m1 = pl.pallas_call(
        functools.partial(_scan_body, reverse=reverse, finalize=finalize is not None, dk=dk, dv=dv),
        out_shape=(jax.ShapeDtypeStruct((t, v_w), out_dtype),
                   jax.ShapeDtypeStruct(c0.shape, F32),
                   jax.ShapeDtypeStruct(n0.shape, F32),
                   jax.ShapeDtypeStruct(m0.shape, F32)),
        grid=(nc,),
        in_specs=in_specs,
        out_specs=(pl.BlockSpec((L, v_w), lambda i: (cidx(i), 0)),
                   full(c0.shape), full(n0.shape), full(m0.shape)),
        compiler_params=_cparams(("arbitrary",)),
        name="mlstm_scan_bwd" if reverse else "mlstm_scan_fwd",
    )(*args)
    return out, (c1, n1, m1)


_CONV_LANES = 512
_CONV_ROWS = 64
_CONV_PAD = 16


def _conv_body(x_ref, w_ref, o_ref, pad_ref, *, seq_len, n_seq):
    zeros = jnp.zeros((_CONV_PAD, _CONV_LANES), F32)
    pad_ref[0:_CONV_PAD, :] = zeros
    pad_ref[_CONV_PAD + seq_len:2 * _CONV_PAD + seq_len, :] = zeros

    def one_seq(s, carry):
        base = pl.multiple_of(s * seq_len, seq_len)
        pad_ref[_CONV_PAD:_CONV_PAD + seq_len, :] = x_ref[pl.ds(base, seq_len), :]
        for r0 in range(0, seq_len, _CONV_ROWS):
            acc = jnp.zeros((_CONV_ROWS, _CONV_LANES), F32)
            for j in range(CONV_W):
                start = _CONV_PAD - CONV_HALF + j + r0
                acc = acc + w_ref[j:j + 1, :] * pad_ref[start:start + _CONV_ROWS, :]
            o_ref[pl.ds(base + r0, _CONV_ROWS), :] = acc
        return carry

    lax.fori_loop(0, n_seq, one_seq, 0)


def _dwconv(x, w_dw, *, seq_len):
    t, d = x.shape
    n_seq = max(1, min(512, t) // seq_len)
    rows = n_seq * seq_len
    assert t % rows == 0 and d % _CONV_LANES == 0 and seq_len % _CONV_ROWS == 0
    return pl.pallas_call(
        functools.partial(_conv_body, seq_len=seq_len, n_seq=n_seq),
        out_shape=jax.ShapeDtypeStruct((t, d), F32),
        grid=(t // rows, d // _CONV_LANES),
        in_specs=[pl.BlockSpec((rows, _CONV_LANES), lambda i, j: (i, j)),
                  pl.BlockSpec((CONV_W, _CONV_LANES), lambda i, j: (0, j))],
        out_specs=pl.BlockSpec((rows, _CONV_LANES), lambda i, j: (i, j)),
        scratch_shapes=[pltpu.VMEM((seq_len + 2 * _CONV_PAD, _CONV_LANES), F32)],
        compiler_params=_cparams(("parallel", "parallel")),
        name="dwconv",
    )(x, w_dw)


def _router_body(x_ref, g_ref, sc_ref, sh_ref, wr_ref, rb_ref, u_ref, gates_ref, *, n_experts):
    x = x_ref[...]
    y = x * lax.rsqrt(jnp.mean(x * x, axis=-1, keepdims=True) + EPS) * g_ref[...]
    u = (y * (1.0 + sc_ref[...]) + sh_ref[...]).astype(BF16)
    u_ref[...] = u
    logits = jnp.dot(u, wr_ref[...].astype(BF16), preferred_element_type=F32)
    scores = _sigmoid(logits)
    lane = lax.broadcasted_iota(jnp.int32, scores.shape, 1)
    sel = jnp.where(lane < n_experts, scores + rb_ref[...], -jnp.inf)
    picked = jnp.zeros_like(scores)
    for _ in range(TOP_K):
        mx = jnp.max(sel, axis=-1, keepdims=True)
        first = jnp.min(jnp.where(sel == mx, lane, LANES), axis=-1, keepdims=True)
        hit = lane == first
        picked = jnp.where(hit, scores, picked)
        sel = jnp.where(hit, -jnp.inf, sel)
    gates_ref[...] = picked / jnp.sum(picked, axis=-1, keepdims=True) * ROUTED_SCALE


def _router(x, g, scale, shift, w_router_pad, router_bias_pad, n_experts, tm=256):
    t, d = x.shape
    tm = min(tm, t)
    vec = lambda n: pl.BlockSpec((1, n), lambda i: (0, 0))
    return pl.pallas_call(
        functools.partial(_router_body, n_experts=n_experts),
        out_shape=(jax.ShapeDtypeStruct((t, d), BF16), jax.ShapeDtypeStruct((t, LANES), F32)),
        grid=(t // tm,),
        in_specs=[pl.BlockSpec((tm, d), lambda i: (i, 0)), vec(d), vec(d), vec(d),
                  pl.BlockSpec((d, LANES), lambda i: (0, 0)), vec(LANES)],
        out_specs=(pl.BlockSpec((tm, d), lambda i: (i, 0)), pl.BlockSpec((tm, LANES), lambda i: (i, 0))),
        compiler_params=_cparams(("parallel",)),
        name="moe_router",
    )(x, g, scale, shift, w_router_pad, router_bias_pad)


def _moe_body(u_ref, gates_ref, wg_ref, wu_ref, wd_ref, swg_ref, swu_ref, swd_ref, o_ref, *, n_experts):
    e = pl.program_id(1)
    u = u_ref[...]

    @pl.when(e == 0)
    def _():
        hs = _silu(jnp.dot(u, swg_ref[...].astype(BF16), preferred_element_type=F32))
        hs = hs * jnp.dot(u, swu_ref[...].astype(BF16), preferred_element_type=F32)
        o_ref[...] = jnp.dot(hs.astype(BF16), swd_ref[...].astype(BF16), preferred_element_type=F32)

    gates = gates_ref[...]
    lane = lax.broadcasted_iota(jnp.int32, gates.shape, 1)
    gate = jnp.sum(jnp.where(lane == e, gates, 0.0), axis=-1, keepdims=True)
    hg = jnp.dot(u, wg_ref[...].astype(BF16), preferred_element_type=F32)
    hu = jnp.dot(u, wu_ref[...].astype(BF16), preferred_element_type=F32)
    hh = (_silu(hg) * hu * gate).astype(BF16)
    o_ref[...] += jnp.dot(hh, wd_ref[...].astype(BF16), preferred_element_type=F32)


def _moe_dense(u, gates, e_w_gate, e_w_up, e_w_down, s_w_gate, s_w_up, s_w_down, tm=512):
    t, d = u.shape
    n_experts, _, di = e_w_gate.shape
    ds_ = s_w_gate.shape[1]
    tm = min(tm, t)
    return pl.pallas_call(
        functools.partial(_moe_body, n_experts=n_experts),
        out_shape=jax.ShapeDtypeStruct((t, d), F32),
        grid=(t // tm, n_experts),
        in_specs=[pl.BlockSpec((tm, d), lambda i, e: (i, 0)),
                  pl.BlockSpec((tm, LANES), lambda i, e: (i, 0)),
                  pl.BlockSpec((None, d, di), lambda i, e: (e, 0, 0)),
                  pl.BlockSpec((None, d, di), lambda i, e: (e, 0, 0)),
                  pl.BlockSpec((None, di, d), lambda i, e: (e, 0, 0)),
                  pl.BlockSpec((d, ds_), lambda i, e: (0, 0)),
                  pl.BlockSpec((d, ds_), lambda i, e: (0, 0)),
                  pl.BlockSpec((ds_, d), lambda i, e: (0, 0))],
        out_specs=pl.BlockSpec((tm, d), lambda i, e: (i, 0)),
        compiler_params=_cparams(("parallel", "arbitrary")),
        name="moe_experts",
    )(u, gates, e_w_gate, e_w_up, e_w_down, s_w_gate, s_w_up, s_w_down)


def _residual_body(*refs, final):
    if final:
        x_ref, y_ref, gate_ref, fn_ref, o_ref = refs
    else:
        x_ref, y_ref, gate_ref, o_ref = refs
    x = x_ref[...] + gate_ref[...] * y_ref[...]
    if final:
        x = x * lax.rsqrt(jnp.mean(x * x, axis=-1, keepdims=True) + EPS) * fn_ref[...]
    o_ref[...] = x


def _residual(x, y, gate, final_norm=None, tm=256):
    t, d = x.shape
    tm = min(tm, t)
    blk = pl.BlockSpec((tm, d), lambda i: (i, 0))
    vec = pl.BlockSpec((1, d), lambda i: (0, 0))
    args = [x, y, gate] + ([] if final_norm is None else [final_norm])
    return pl.pallas_call(
        functools.partial(_residual_body, final=final_norm is not None),
        out_shape=jax.ShapeDtypeStruct((t, d), F32),
        grid=(t // tm,),
        in_specs=[blk, blk, vec] + ([] if final_norm is None else [vec]),
        out_specs=blk,
        compiler_params=_cparams(("parallel",)),
        name="residual",
    )(*args)


def _pad_cols(a, n):
    return jnp.pad(a, ((0, 0), (0, n - a.shape[1])))


def _mlstm_layer(streams, mods, norm_g, w_in, gate_bias, head_norm, w_out, ctx_out):
    H = M_HEADS
    d = w_in.shape[0]
    dv = head_norm.shape[0] // H
    dk = dv // 2
    n_proj = 2 * H * dk + 2 * H * dv
    w_gates = _pad_cols(w_in[:, n_proj:], LANES)
    bias_row = _pad_cols(gate_bias.reshape(1, 4 * H), LANES)
    bias_col = bias_row.reshape(LANES, 1)
    hnorm = head_norm.reshape(1, H * dv)

    projs = []
    for xs, row in streams:
        sh, sc = mods[row][0], mods[row][1]
        pro = (norm_g, sc, sh)
        proj = _fused_matmul(xs, w_in, n_out=n_proj, prologue="mod", pro_args=pro,
                             out_dtype=BF16, name="mlstm_in_proj")
        gates = _fused_matmul(xs, w_gates, n_out=LANES, prologue="mod", pro_args=pro,
                              out_dtype=F32, name="mlstm_gate_proj")
        projs.append((proj, gates, gates.T))

    zero = (jnp.zeros((H, dk, dv), F32), jnp.zeros((H, 1, dk), F32), jnp.zeros((H, 1, LANES), F32))
    (px, gx, gxt), (pc, gc, gct) = projs
    hcf, st_f = _mlstm_scan(pc, gc, gct, bias_row, bias_col, zero, reverse=False)
    hxf, _ = _mlstm_scan(px, gx, gxt, bias_row, bias_col, st_f, reverse=False)
    yc_pre, st_b = _mlstm_scan(pc, gc, gct, bias_row, bias_col, zero, reverse=True, finalize=(hcf, hnorm))
    yx_pre, _ = _mlstm_scan(px, gx, gxt, bias_row, bias_col, st_b, reverse=True, finalize=(hxf, hnorm))

    outs = []
    for (xs, row), y_pre, live in zip(streams, (yx_pre, yc_pre), (True, ctx_out)):
        if live:
            outs.append(_fused_matmul(y_pre, w_out, n_out=d, epilogue="residual",
                                      epi_args=(xs, mods[row][2]), name="mlstm_out_proj"))
        else:
            outs.append(xs)
    return outs


def _conv_layer(streams, mods, norm_g, w_in, w_dw, ln_g, ln_b, w_out, live_flags, seq_lens):
    d = w_in.shape[0]
    outs = []
    for (xs, row), live, seq_len in zip(streams, live_flags, seq_lens):
        if not live:
            outs.append(xs)
            continue
        sh, sc, gate = mods[row][0], mods[row][1], mods[row][2]
        hglu = _fused_matmul(xs, w_in, n_out=d, prologue="mod", pro_args=(norm_g, sc, sh),
                             epilogue="glu", w2_col_offset=d, tn=256, name="conv_in_glu")
        hc = _dwconv(hglu, w_dw, seq_len=seq_len)
        outs.append(_fused_matmul(hc, w_out, n_out=d, prologue="ln_silu", pro_args=(ln_g, ln_b),
                                  epilogue="residual", epi_args=(xs, gate), name="conv_out_proj"))
    return outs


def kernel(x, c, ctx, c_ctx, ada_down, ada_up, ada_bias, norm_mix, norm_ffn, a_w_in, a_gate_bias, a_head_norm, a_w_out, b_w_in, b_w_dw, b_ln_g, b_ln_b, b_w_out, w_router, router_bias, e_w_gate, e_w_up, e_w_down, s_w_gate, s_w_up, s_w_down, final_norm):
    bsz, t, d = x.shape
    assert bsz == 1 and c.shape[0] == 1 and ctx.shape[0] == 1
    depth = ada_down.shape[0]
    n_experts = w_router.shape[-1]
    t_ctx = ctx.shape[1]

    cond = jnp.zeros((SUBLANES, d), F32).at[0].set(c[0]).at[1].set(c_ctx)
    mods_all = _ada_all(cond, ada_down, ada_up, ada_bias)

    xs, cs = x[0], ctx[0]
    for i in range(depth):
        kind, j = i % 2, i // 2
        ctx_live = any(l % 2 == 0 for l in range(i + 1, depth))
        need_ctx = kind == 0 or ctx_live
        mods = [[mods_all[i, r:r + 1, k * d:(k + 1) * d] for k in range(N_MOD)] for r in range(2)]
        norm_g = norm_mix[i].reshape(1, d)
        streams = [(xs, 0), (cs, 1)]
        if kind == 0:
            xs, cs = _mlstm_layer(streams, mods, norm_g, a_w_in[j], a_gate_bias[j], a_head_norm[j],
                                  a_w_out[j], ctx_live)
        else:
            xs, cs = _conv_layer(streams, mods, norm_g, b_w_in[j], b_w_dw[j],
                                 b_ln_g[j].reshape(1, d), b_ln_b[j].reshape(1, d), b_w_out[j],
                                 (True, ctx_live), (GRID_W, t_ctx))
        del need_ctx

        wr = _pad_cols(w_router[i], LANES)
        rb = _pad_cols(router_bias[i].reshape(1, n_experts), LANES)
        ffn_g = norm_ffn[i].reshape(1, d)
        last = i == depth - 1
        new = []
        for (s, row), live in zip(((xs, 0), (cs, 1)), (True, ctx_live)):
            if not live:
                new.append(s)
                continue
            u, gates = _router(s, ffn_g, mods[row][4], mods[row][3], wr, rb, n_experts)
            moe = _moe_dense(u, gates, e_w_gate[i].astype(BF16), e_w_up[i].astype(BF16),
                             e_w_down[i].astype(BF16), s_w_gate[i].astype(BF16),
                             s_w_up[i].astype(BF16), s_w_down[i].astype(BF16))
            fn = final_norm.reshape(1, d) if (last and row == 0) else None
            new.append(_residual(s, moe, mods[row][5], fn))
        xs, cs = new
    return xs[None]
```

```python
import functools

import jax
import jax.numpy as jnp
from jax import lax
from jax.experimental import pallas as pl
from jax.experimental.pallas import tpu as pltpu

F32 = jnp.float32
BF16 = jnp.bfloat16
I32 = jnp.int32

GRID_W = 64
M_HEADS = 8
M_CHUNK = 128
CONV_W = 31
CONV_HALF = CONV_W // 2
TOP_K = 6
ROUTED_SCALE = 2.5
N_MOD = 6
EPS = 1e-6
SH1, SC1, G1, SH2, SC2, G2 = range(N_MOD)

LANES = 128
SUBLANES = 8
VMEM_LIMIT = 56 * 1024 * 1024
HIGHEST = lax.Precision.HIGHEST


def _cparams(sem):
    return pltpu.CompilerParams(dimension_semantics=sem, vmem_limit_bytes=VMEM_LIMIT)


def _sigmoid(v):
    return 1.0 / (1.0 + jnp.exp(-v))


def _silu(v):
    return v * _sigmoid(v)


def _log_sigmoid(v):
    return jnp.minimum(v, 0.0) - jnp.log(1.0 + jnp.exp(-jnp.abs(v)))


def _rmsnorm(x, g):
    return x * lax.rsqrt(jnp.mean(x * x, axis=-1, keepdims=True) + EPS) * g


def _ada_body(cond_ref, down_ref, up_ref, bias_ref, o_ref):
    s = _silu(cond_ref[...])
    t = jnp.dot(s, down_ref[...], precision=HIGHEST, preferred_element_type=F32)
    o_ref[...] = jnp.dot(t, up_ref[...], precision=HIGHEST, preferred_element_type=F32) + bias_ref[...]


def _ada_all(cond, ada_down, ada_up, ada_bias):
    n_layers, d, rank = ada_down.shape
    n_out = ada_up.shape[-1]
    tn = d
    assert n_out % tn == 0
    return pl.pallas_call(
        _ada_body,
        out_shape=jax.ShapeDtypeStruct((n_layers, SUBLANES, n_out), F32),
        grid=(n_layers, n_out // tn),
        in_specs=[
            pl.BlockSpec((SUBLANES, d), lambda l, j: (0, 0)),
            pl.BlockSpec((None, d, rank), lambda l, j: (l, 0, 0)),
            pl.BlockSpec((None, rank, tn), lambda l, j: (l, 0, j)),
            pl.BlockSpec((None, 1, tn), lambda l, j: (l, 0, j)),
        ],
        out_specs=pl.BlockSpec((None, SUBLANES, tn), lambda l, j: (l, 0, j)),
        compiler_params=_cparams(("parallel", "arbitrary")),
        name="ada_mod",
    )(cond, ada_down, ada_up, ada_bias.reshape(n_layers, 1, n_out))


_PRO_ROWS = 32


def _mm_body(*refs, prologue, epilogue, extra, tm, row, d_mod):
    refs = list(refs)
    a_ref = refs.pop(0)
    if prologue == "mod":
        g_ref, mods_ref = refs.pop(0), refs.pop(0)
    elif prologue == "ln_silu":
        g_ref, b_ref = refs.pop(0), refs.pop(0)
    w_ref = refs.pop(0)
    if epilogue == "glu":
        w2_ref = refs.pop(0)
    elif epilogue == "residual":
        res_ref, gate_ref = refs.pop(0), refs.pop(0)
    if extra:
        wx_ref = refs.pop(0)
    o_ref = refs.pop(0)
    if extra:
        ox_ref = refs.pop(0)

    if prologue == "none":
        u = a_ref[...]
    else:
        u_ref = refs.pop(0)

        @pl.when(pl.program_id(1) == 0)
        def _():
            def rows(r, carry):
                sl = pl.ds(pl.multiple_of(r * _PRO_ROWS, _PRO_ROWS), _PRO_ROWS)
                x = a_ref[sl, :]
                if prologue == "mod":
                    shift = mods_ref[row:row + 1, SH1 * d_mod:(SH1 + 1) * d_mod]
                    scale = mods_ref[row:row + 1, SC1 * d_mod:(SC1 + 1) * d_mod]
                    y = _rmsnorm(x, g_ref[...]) * (1.0 + scale) + shift
                else:
                    mu = jnp.mean(x, axis=-1, keepdims=True)
                    xc = x - mu
                    var = jnp.mean(xc * xc, axis=-1, keepdims=True)
                    y = _silu(xc * lax.rsqrt(var + EPS) * g_ref[...] + b_ref[...])
                u_ref[sl, :] = y.astype(BF16)
                return carry

            lax.fori_loop(0, tm // _PRO_ROWS, rows, 0)

        u = u_ref[...]

    if extra:
        @pl.when(pl.program_id(1) == 0)
        def _():
            ox_ref[...] = jnp.dot(u, wx_ref[...].astype(BF16), preferred_element_type=F32)

    acc = jnp.dot(u, w_ref[...], preferred_element_type=F32)
    if epilogue == "glu":
        acc2 = jnp.dot(u, w2_ref[...], preferred_element_type=F32)
        acc = acc * _sigmoid(acc2)
    elif epilogue == "residual":
        acc = res_ref[...] + gate_ref[row:row + 1, :] * acc
    o_ref[...] = acc.astype(o_ref.dtype)


def _fused_matmul(a, w, *, n_out, prologue="none", norm=None, ln=None, mods=None, row=0,
                  epilogue="plain", res=None, w2_col_offset=0, extra_w=None,
                  out_dtype=F32, tm=1024, tn=512, name="mm"):
    m, k = a.shape
    tm = min(tm, m)
    tn = min(tn, n_out)
    assert m % tm == 0 and n_out % tn == 0 and w.shape[0] == k and w.dtype == BF16 and tm % _PRO_ROWS == 0
    d_mod = None
    a_mode = {} if prologue == "none" else dict(pipeline_mode=pl.Buffered(1))
    in_specs = [pl.BlockSpec((tm, k), lambda i, j: (i, 0), **a_mode)]
    args = [a]
    if prologue == "mod":
        norm_arr, norm_layer = norm
        mods_all, mods_layer = mods
        d_mod = mods_all.shape[-1] // N_MOD
        in_specs += [pl.BlockSpec((None, 1, k), lambda i, j: (norm_layer, 0, 0)),
                     pl.BlockSpec((None, SUBLANES, mods_all.shape[-1]), lambda i, j: (mods_layer, 0, 0))]
        args += [norm_arr, mods_all]
    elif prologue == "ln_silu":
        (ln_g, ln_b), ln_layer = ln
        in_specs += [pl.BlockSpec((None, 1, k), lambda i, j: (ln_layer, 0, 0))] * 2
        args += [ln_g, ln_b]
    in_specs.append(pl.BlockSpec((k, tn), lambda i, j: (0, j)))
    args.append(w)
    if epilogue == "glu":
        off = w2_col_offset // tn
        in_specs.append(pl.BlockSpec((k, tn), lambda i, j: (0, j + off)))
        args.append(w)
    elif epilogue == "residual":
        mods_all, mods_layer = mods
        goff = G1 * (mods_all.shape[-1] // N_MOD) // tn
        in_specs += [pl.BlockSpec((tm, tn), lambda i, j: (i, j)),
                     pl.BlockSpec((None, SUBLANES, tn), lambda i, j: (mods_layer, 0, goff + j))]
        args += [res, mods_all]
    out_shape = [jax.ShapeDtypeStruct((m, n_out), out_dtype)]
    out_specs = [pl.BlockSpec((tm, tn), lambda i, j: (i, j))]
    if extra_w is not None:
        in_specs.append(pl.BlockSpec((k, LANES), lambda i, j: (0, 0)))
        args.append(extra_w)
        out_shape.append(jax.ShapeDtypeStruct((m, LANES), F32))
        out_specs.append(pl.BlockSpec((tm, LANES), lambda i, j: (i, 0)))
    scratch = [] if prologue == "none" else [pltpu.VMEM((tm, k), BF16)]
    outs = pl.pallas_call(
        functools.partial(_mm_body, prologue=prologue, epilogue=epilogue, extra=extra_w is not None,
                          tm=tm, row=row, d_mod=d_mod),
        out_shape=out_shape,
        grid=(m // tm, n_out // tn),
        in_specs=in_specs,
        out_specs=out_specs,
        scratch_shapes=scratch,
        compiler_params=_cparams(("parallel", "arbitrary")),
        name=name,
    )(*args)
    return outs if extra_w is not None else outs[0]


def _scan_body(*refs, reverse, finalize, dk, dv):
    refs = list(refs)
    q_ref, k_ref, v_ref = refs.pop(0), refs.pop(0), refs.pop(0)
    gcol_ref, grow_ref, bias_row_ref, bias_col_ref = refs.pop(0), refs.pop(0), refs.pop(0), refs.pop(0)
    c0_ref, n0_ref, m0_ref = refs.pop(0), refs.pop(0), refs.pop(0)
    if finalize:
        o_ref, hprev_ref, hnorm_ref = refs.pop(0), refs.pop(0), refs.pop(0)
    out_ref, c_ref, n_ref, m_ref = refs.pop(0), refs.pop(0), refs.pop(0), refs.pop(0)

    L = M_CHUNK
    H = M_HEADS

    @pl.when(pl.program_id(0) == 0)
    def _():
        c_ref[...] = c0_ref[...]
        n_ref[...] = n0_ref[...]
        m_ref[...] = m0_ref[...]

    kind_i = 2 if reverse else 0
    kind_f = kind_i + 1

    t_idx = lax.broadcasted_iota(I32, (L, L), 0)
    s_idx = lax.broadcasted_iota(I32, (L, L), 1)
    mask = (s_idx >= t_idx) if reverse else (s_idx <= t_idx)
    mask_f = mask.astype(F32)
    mask_t = ((t_idx >= s_idx) if reverse else (t_idx <= s_idx)).astype(F32)
    q_scale = dk ** -0.5

    gcol = gcol_ref[...] + bias_row_ref[...]
    grow = grow_ref[...] + bias_col_ref[...]
    lf_col = _log_sigmoid(gcol)
    lf_row = _log_sigmoid(grow[kind_f * H:(kind_f + 1) * H, :])
    b_col_all = jnp.dot(mask_f, lf_col, precision=HIGHEST, preferred_element_type=F32)
    b_row_all = jnp.dot(lf_row, mask_t, precision=HIGHEST, preferred_element_type=F32)

    for h in range(H):
        ci, cf = kind_i * H + h, kind_f * H + h
        b_col = b_col_all[:, cf:cf + 1]
        b_row = b_row_all[h:h + 1, :]
        i_col = gcol[:, ci:ci + 1]
        i_row = grow[ci:ci + 1, :]
        m_prev = m_ref[h][:, :1]
        q = q_ref[:, h * dk:(h + 1) * dk]
        k = k_ref[:, h * dk:(h + 1) * dk]
        v = v_ref[:, h * dv:(h + 1) * dv]
        c_state = c_ref[h]
        n_state = n_ref[h]

        dmat = jnp.where(mask, b_col - b_row + i_row, -jnp.inf)
        m_inter = b_col + m_prev
        m_t = jnp.maximum(m_inter, jnp.max(dmat, axis=-1, keepdims=True))
        w_intra = jnp.exp(dmat - m_t) * q_scale
        w_inter = jnp.exp(m_inter - m_t) * q_scale
        s = lax.dot_general(q, k, (((1,), (1,)), ((), ())), preferred_element_type=F32) * w_intra
        num = jnp.dot(s.astype(BF16), v, preferred_element_type=F32)
        num = num + w_inter * jnp.dot(q, c_state.astype(BF16), preferred_element_type=F32)
        qn = jnp.sum(q.astype(F32) * n_state, axis=-1, keepdims=True)
        den = jnp.sum(s, axis=-1, keepdims=True) + w_inter * qn
        hv = num * (1.0 / jnp.maximum(jnp.abs(den), jnp.exp(-m_t)))

        b_last = jnp.sum(lf_row[h:h + 1, :], axis=-1, keepdims=True)
        m_new = jnp.maximum(b_last + m_prev, jnp.max(b_last - b_row + i_row, axis=-1, keepdims=True))
        kw = k.astype(F32) * jnp.exp(b_last - b_col + i_col - m_new)
        decay = jnp.exp(b_last + m_prev - m_new)
        c_ref[h] = decay * c_state + lax.dot_general(
            kw.astype(BF16), v, (((0,), (0,)), ((), ())), preferred_element_type=F32)
        n_ref[h] = decay * n_state + jnp.sum(kw, axis=0, keepdims=True)
        m_ref[h] = jnp.broadcast_to(m_new, (1, LANES))

        cols = slice(h * dv, (h + 1) * dv)
        if finalize:
            ht = hv + hprev_ref[:, cols]
            ht = ht * lax.rsqrt(jnp.mean(ht * ht, axis=-1, keepdims=True) + EPS)
            ht = ht * hnorm_ref[:, cols] * _sigmoid(o_ref[:, cols].astype(F32))
            out_ref[:, cols] = ht.astype(out_ref.dtype)
        else:
            out_ref[:, cols] = hv


def _mlstm_scan(proj, gcol, grow, bias_row, bias_col, state, *, reverse, finalize=None):
    t = proj.shape[0]
    H, L = M_HEADS, M_CHUNK
    c0, n0, m0 = state
    dk, dv = c0.shape[1], c0.shape[2]
    nc = t // L
    cidx = (lambda i: nc - 1 - i) if reverse else (lambda i: i)
    qk_w, v_w = H * dk, H * dv
    assert 2 * qk_w == v_w
    full = lambda shape: pl.BlockSpec(shape, lambda i: (0,) * len(shape))
    in_specs = [
        pl.BlockSpec((L, qk_w), lambda i: (cidx(i), 0)),
        pl.BlockSpec((L, qk_w), lambda i: (cidx(i), 1)),
        pl.BlockSpec((L, v_w), lambda i: (cidx(i), 1)),
        pl.BlockSpec((L, LANES), lambda i: (cidx(i), 0)),
        pl.BlockSpec((LANES, L), lambda i: (0, cidx(i))),
        full((1, LANES)), full((LANES, 1)),
        full(c0.shape), full(n0.shape), full(m0.shape),
    ]
    args = [proj, proj, proj, gcol, grow, bias_row, bias_col, c0, n0, m0]
    if finalize is not None:
        hprev, (hnorm, hn_layer) = finalize
        in_specs += [pl.BlockSpec((L, v_w), lambda i: (cidx(i), 2)),
                     pl.BlockSpec((L, v_w), lambda i: (cidx(i), 0)),
                     pl.BlockSpec((None, 1, v_w), lambda i: (hn_layer, 0, 0))]
        args += [proj, hprev, hnorm]
    out_dtype = BF16 if finalize is not None else F32
    out, c1, n1, m1 = pl.pallas_call(
        functools.partial(_scan_body, reverse=reverse, finalize=finalize is not None, dk=dk, dv=dv),
        out_shape=(jax.ShapeDtypeStruct((t, v_w), out_dtype),
                   jax.ShapeDtypeStruct(c0.shape, F32),
                   jax.ShapeDtypeStruct(n0.shape, F32),
                   jax.ShapeDtypeStruct(m0.shape, F32)),
        grid=(nc,),
        in_specs=in_specs,
        out_specs=(pl.BlockSpec((L, v_w), lambda i: (cidx(i), 0)),
                   full(c0.shape), full(n0.shape), full(m0.shape)),
        compiler_params=_cparams(("arbitrary",)),
        name="mlstm_scan_bwd" if reverse else "mlstm_scan_fwd",
    )(*args)
    return out, (c1, n1, m1)


_CONV_LANES = 512
_CONV_ROWS = 64
_CONV_PAD = 16


def _conv_body(x_ref, w_ref, o_ref, pad_ref, shift_ref, *, seq_len, n_seq):
    zeros = jnp.zeros((_CONV_PAD, _CONV_LANES), F32)
    pad_ref[0:_CONV_PAD, :] = zeros
    pad_ref[_CONV_PAD + seq_len:2 * _CONV_PAD + seq_len, :] = zeros
    n_shift_rows = shift_ref.shape[1]

    def one_seq(s, carry):
        base = pl.multiple_of(s * seq_len, seq_len)
        pad_ref[_CONV_PAD:_CONV_PAD + seq_len, :] = x_ref[pl.ds(base, seq_len), :]
        for b in range(1, SUBLANES):
            shift_ref[b] = pad_ref[b:b + n_shift_rows, :]
        for r0 in range(0, seq_len, _CONV_ROWS):
            acc = jnp.zeros((_CONV_ROWS, _CONV_LANES), F32)
            for j in range(CONV_W):
                start = _CONV_PAD - CONV_HALF + j + r0
                b, a = start % SUBLANES, start - start % SUBLANES
                src = pad_ref if b == 0 else shift_ref.at[b]
                acc = acc + w_ref[j:j + 1, :] * src[a:a + _CONV_ROWS, :]
            o_ref[pl.ds(base + r0, _CONV_ROWS), :] = acc
        return carry

    lax.fori_loop(0, n_seq, one_seq, 0)


def _dwconv(x, w_dw, layer, *, seq_len):
    t, d = x.shape
    n_seq = max(1, min(512, t) // seq_len)
    rows = n_seq * seq_len
    assert t % rows == 0 and d % _CONV_LANES == 0 and seq_len % _CONV_ROWS == 0
    return pl.pallas_call(
        functools.partial(_conv_body, seq_len=seq_len, n_seq=n_seq),
        out_shape=jax.ShapeDtypeStruct((t, d), F32),
        grid=(t // rows, d // _CONV_LANES),
        in_specs=[pl.BlockSpec((rows, _CONV_LANES), lambda i, j: (i, j)),
                  pl.BlockSpec((None, CONV_W, _CONV_LANES), lambda i, j: (layer, 0, j))],
        out_specs=pl.BlockSpec((rows, _CONV_LANES), lambda i, j: (i, j)),
        scratch_shapes=[pltpu.VMEM((seq_len + 2 * _CONV_PAD, _CONV_LANES), F32),
                        pltpu.VMEM((SUBLANES, seq_len + 2 * _CONV_PAD - SUBLANES, _CONV_LANES), F32)],
        compiler_params=_cparams(("parallel", "parallel")),
        name="dwconv",
    )(x, w_dw)


META_IDX, META_RANK, META_W = 0, 8, 16


def _router_body(x_ref, g_ref, mods_ref, wr_ref, rb_ref, swg_ref, swu_ref, swd_ref,
                 u_ref, xb_ref, meta_ref, counts_ref, carry_ref, *, n_experts, row, d, tm):
    @pl.when(pl.program_id(0) == 0)
    def _():
        carry_ref[...] = jnp.zeros_like(carry_ref)

    x = x_ref[...]
    shift = mods_ref[row:row + 1, SH2 * d:(SH2 + 1) * d]
    scale = mods_ref[row:row + 1, SC2 * d:(SC2 + 1) * d]
    gate2 = mods_ref[row:row + 1, G2 * d:(G2 + 1) * d]
    uf = _rmsnorm(x, g_ref[...]) * (1.0 + scale) + shift
    u = uf.astype(BF16)
    u_ref[...] = u.astype(F32)

    hs = _silu(jnp.dot(u, swg_ref[...].astype(BF16), preferred_element_type=F32))
    hs = hs * jnp.dot(u, swu_ref[...].astype(BF16), preferred_element_type=F32)
    shared = jnp.dot(hs.astype(BF16), swd_ref[...].astype(BF16), preferred_element_type=F32)
    xb_ref[...] = x + gate2 * shared

    logits = jnp.dot(u, wr_ref[...].astype(BF16), preferred_element_type=F32)
    scores = _sigmoid(logits)
    lane = lax.broadcasted_iota(I32, scores.shape, 1)
    sel = jnp.where(lane < n_experts, scores + rb_ref[...], -jnp.inf)
    picked = jnp.zeros_like(scores)
    chosen = jnp.zeros_like(scores)
    firsts = []
    for _ in range(TOP_K):
        mx = jnp.max(sel, axis=-1, keepdims=True)
        first = jnp.min(jnp.where(sel == mx, lane, LANES), axis=-1, keepdims=True)
        hit = lane == first
        picked = jnp.where(hit, scores, picked)
        chosen = jnp.where(hit, 1.0, chosen)
        sel = jnp.where(hit, -jnp.inf, sel)
        firsts.append(first)
    gates = picked / jnp.sum(picked, axis=-1, keepdims=True) * ROUTED_SCALE

    r_idx = lax.broadcasted_iota(I32, (tm, tm), 0)
    c_idx = lax.broadcasted_iota(I32, (tm, tm), 1)
    before = (c_idx < r_idx).astype(BF16)
    pos = jnp.dot(before, chosen.astype(BF16), preferred_element_type=F32) + carry_ref[...]
    meta = jnp.zeros_like(scores)
    for kk, first in enumerate(firsts):
        hit = lane == first
        rank = jnp.sum(jnp.where(hit, pos, 0.0), axis=-1, keepdims=True)
        wk = jnp.sum(jnp.where(hit, gates, 0.0), axis=-1, keepdims=True)
        meta = jnp.where(lane == META_IDX + kk, first.astype(F32), meta)
        meta = jnp.where(lane == META_RANK + kk, rank, meta)
        meta = jnp.where(lane == META_W + kk, wk, meta)
    meta_ref[...] = meta
    carry_ref[...] += jnp.sum(chosen, axis=0, keepdims=True)
    counts_ref[...] = carry_ref[...]


def _router(x, norm_ffn3, layer, mods_all, row, wr_pad, rb_pad, s_w_gate, s_w_up, s_w_down, n_experts, tm=256):
    t, d = x.shape
    tm = min(tm, t)
    ds_ = s_w_gate.shape[-1]
    c2 = lambda i: (0, 0)
    lay3 = lambda i: (layer, 0, 0)
    blk = pl.BlockSpec((tm, d), lambda i: (i, 0))
    return pl.pallas_call(
        functools.partial(_router_body, n_experts=n_experts, row=row, d=d, tm=tm),
        out_shape=(jax.ShapeDtypeStruct((t, d), F32), jax.ShapeDtypeStruct((t, d), F32),
                   jax.ShapeDtypeStruct((t, LANES), F32), jax.ShapeDtypeStruct((1, LANES), F32)),
        grid=(t // tm,),
        in_specs=[blk,
                  pl.BlockSpec((None, 1, d), lay3),
                  pl.BlockSpec((None, SUBLANES, mods_all.shape[-1]), lay3),
                  pl.BlockSpec((d, LANES), c2), pl.BlockSpec((1, LANES), c2),
                  pl.BlockSpec((None, d, ds_), lay3), pl.BlockSpec((None, d, ds_), lay3),
                  pl.BlockSpec((None, ds_, d), lay3)],
        out_specs=(blk, blk, pl.BlockSpec((tm, LANES), lambda i: (i, 0)), pl.BlockSpec((1, LANES), c2)),
        scratch_shapes=[pltpu.VMEM((1, LANES), F32)],
        compiler_params=_cparams(("arbitrary",)),
        name="moe_router",
    )(x, norm_ffn3, mods_all, wr_pad, rb_pad, s_w_gate, s_w_up, s_w_down)


def _dispatch_body(slot_ref, u_hbm, xs_hbm, sem, *, tokens):
    tok0 = pl.program_id(0) * tokens

    def issue(t, carry):
        for kk in range(TOP_K):
            s = slot_ref[0, t * TOP_K + kk]
            pltpu.make_async_copy(u_hbm.at[pl.ds(tok0 + t, 1), :], xs_hbm.at[pl.ds(s, 1), :], sem).start()
        return carry

    lax.fori_loop(0, tokens, issue, 0)
    for _ in range(TOP_K):
        pltpu.make_async_copy(xs_hbm.at[pl.ds(0, tokens), :], xs_hbm.at[pl.ds(0, tokens), :], sem).wait()


def _dispatch(u, slot3, tokens):
    t, d = u.shape
    return pl.pallas_call(
        functools.partial(_dispatch_body, tokens=tokens),
        out_shape=jax.ShapeDtypeStruct((t * TOP_K, d), F32),
        grid=(t // tokens,),
        in_specs=[pl.BlockSpec((None, 1, tokens * TOP_K), lambda i: (i, 0, 0), memory_space=pltpu.SMEM),
                  pl.BlockSpec(memory_space=pl.ANY)],
        out_specs=pl.BlockSpec(memory_space=pl.ANY),
        scratch_shapes=[pltpu.SemaphoreType.DMA],
        compiler_params=_cparams(("arbitrary",)),
        name="moe_dispatch",
    )(slot3, u)


_FFN_ROWS = 256


def _ffn_body(vt_ref, ve_ref, starts_ref, nvis_ref, xs_ref, wg_ref, wu_ref, wd_ref, ys_ref):
    v = pl.program_id(0)

    @pl.when(v < nvis_ref[0])
    def _():
        tile = vt_ref[v]
        e = ve_ref[v]
        x = xs_ref[...].astype(BF16)
        hg = jnp.dot(x, wg_ref[...].astype(BF16), preferred_element_type=F32)
        hu = jnp.dot(x, wu_ref[...].astype(BF16), preferred_element_type=F32)
        hh = (_silu(hg) * hu).astype(BF16)
        y = jnp.dot(hh, wd_ref[...].astype(BF16), preferred_element_type=F32)
        first = jnp.logical_or(v == 0, vt_ref[jnp.maximum(v - 1, 0)] != tile)

        @pl.when(first)
        def _():
            ys_ref[...] = y

        @pl.when(jnp.logical_not(first))
        def _():
            r = lax.broadcasted_iota(I32, (_FFN_ROWS, 1), 0) + tile * _FFN_ROWS
            mine = jnp.logical_and(r >= starts_ref[e], r < starts_ref[e + 1])
            ys_ref[...] = jnp.where(mine, y, ys_ref[...])


def _grouped_ffn(xs, vt, ve, starts, nvis, e_w_gate, e_w_up, e_w_down, layer):
    ns, d = xs.shape
    di = e_w_gate.shape[-1]
    nv = vt.shape[0]
    grid_spec = pltpu.PrefetchScalarGridSpec(
        num_scalar_prefetch=4,
        grid=(nv,),
        in_specs=[pl.BlockSpec((_FFN_ROWS, d), lambda v, vt, ve, st, nn: (vt[v], 0)),
                  pl.BlockSpec((None, None, d, di), lambda v, vt, ve, st, nn: (layer, ve[v], 0, 0)),
                  pl.BlockSpec((None, None, d, di), lambda v, vt, ve, st, nn: (layer, ve[v], 0, 0)),
                  pl.BlockSpec((None, None, di, d), lambda v, vt, ve, st, nn: (layer, ve[v], 0, 0))],
        out_specs=pl.BlockSpec((_FFN_ROWS, d), lambda v, vt, ve, st, nn: (vt[v], 0)),
    )
    return pl.pallas_call(
        _ffn_body,
        out_shape=jax.ShapeDtypeStruct((ns, d), F32),
        grid_spec=grid_spec,
        compiler_params=_cparams(("arbitrary",)),
        name="moe_grouped_ffn",
    )(vt, ve, starts, nvis, xs, e_w_gate, e_w_up, e_w_down)


_COMB_TOKENS = 128


def _combine_body(*refs, row, d, final):
    refs = list(refs)
    slot_ref, meta_ref, xb_ref, mods_ref = refs.pop(0), refs.pop(0), refs.pop(0), refs.pop(0)
    fn_ref = refs.pop(0) if final else None
    ys_hbm, o_ref, gbuf, sem = refs

    def issue(t, carry):
        for kk in range(TOP_K):
            s = slot_ref[0, t * TOP_K + kk]
            pltpu.make_async_copy(ys_hbm.at[pl.ds(s, 1), :], gbuf.at[kk, pl.ds(t, 1), :], sem).start()
        return carry

    lax.fori_loop(0, _COMB_TOKENS, issue, 0)
    for kk in range(TOP_K):
        pltpu.make_async_copy(ys_hbm.at[pl.ds(0, _COMB_TOKENS), :], gbuf.at[kk], sem).wait()

    meta = meta_ref[...]
    acc = meta[:, META_W:META_W + 1] * gbuf[0]
    for kk in range(1, TOP_K):
        acc = acc + meta[:, META_W + kk:META_W + kk + 1] * gbuf[kk]
    out = xb_ref[...] + mods_ref[row:row + 1, G2 * d:(G2 + 1) * d] * acc
    if final:
        out = _rmsnorm(out, fn_ref[...])
    o_ref[...] = out


def _combine(ys, slot3, meta, xb, mods_all, layer, row, final_norm=None):
    t, d = xb.shape
    tc = _COMB_TOKENS
    assert t % tc == 0
    blk = pl.BlockSpec((tc, d), lambda i: (i, 0))
    in_specs = [pl.BlockSpec((None, 1, tc * TOP_K), lambda i: (i, 0, 0), memory_space=pltpu.SMEM),
                pl.BlockSpec((tc, LANES), lambda i: (i, 0)), blk,
                pl.BlockSpec((None, SUBLANES, mods_all.shape[-1]), lambda i: (layer, 0, 0))]
    args = [slot3, meta, xb, mods_all]
    if final_norm is not None:
        in_specs.append(pl.BlockSpec((1, d), lambda i: (0, 0)))
        args.append(final_norm)
    in_specs.append(pl.BlockSpec(memory_space=pl.ANY))
    args.append(ys)
    return pl.pallas_call(
        functools.partial(_combine_body, row=row, d=d, final=final_norm is not None),
        out_shape=jax.ShapeDtypeStruct((t, d), F32),
        grid=(t // tc,),
        in_specs=in_specs,
        out_specs=blk,
        scratch_shapes=[pltpu.VMEM((TOP_K, tc, d), F32), pltpu.SemaphoreType.DMA],
        compiler_params=_cparams(("arbitrary",)),
        name="moe_combine",
    )(*args)


def _moe_layer(x, layer, row, mods_all, norm_ffn3, wr_pad, rb_pad, e_w_gate, e_w_up, e_w_down,
               s_w_gate, s_w_up, s_w_down, final_norm):
    t, d = x.shape
    n_experts = e_w_gate.shape[1]
    u, xb, meta, counts = _router(x, norm_ffn3, layer, mods_all, row, wr_pad, rb_pad,
                                  s_w_gate, s_w_up, s_w_down, n_experts)
    counts = counts[0, :n_experts].astype(I32)
    ends = jnp.cumsum(counts)
    starts = ends - counts
    idx = meta[:, META_IDX:META_IDX + TOP_K].astype(I32)
    rank = meta[:, META_RANK:META_RANK + TOP_K].astype(I32)
    slot = (starts[idx] + rank).reshape(-1)
    ns = t * TOP_K
    n_tiles = ns // _FFN_ROWS
    first_tile = starts // _FFN_ROWS
    ntile_e = jnp.where(counts > 0, (ends - 1) // _FFN_ROWS - first_tile + 1, 0)
    vend = jnp.cumsum(ntile_e)
    vstart = vend - ntile_e
    nvis = vend[-1]
    nv = n_tiles + n_experts - 1
    v = jnp.minimum(jnp.arange(nv, dtype=I32), nvis - 1)
    ve = jnp.clip(jnp.searchsorted(vend, v, side="right"), 0, n_experts - 1).astype(I32)
    vt = (first_tile[ve] + v - vstart[ve]).astype(I32)
    starts_ext = jnp.concatenate([starts, ends[-1:]]).astype(I32)

    disp_tokens = min(256, t)
    xs = _dispatch(u, slot.reshape(t // disp_tokens, 1, disp_tokens * TOP_K), disp_tokens)
    ys = _grouped_ffn(xs, vt, ve, starts_ext, nvis.reshape(1).astype(I32), e_w_gate, e_w_up, e_w_down, layer)
    return _combine(ys, slot.reshape(t // _COMB_TOKENS, 1, _COMB_TOKENS * TOP_K), meta, xb,
                    mods_all, layer, row, final_norm)


def _pad_cols(a, n):
    return jnp.pad(a, ((0, 0), (0, n - a.shape[1])))


def _mlstm_layer(streams, mods, norm, a_w_in, j, gate_bias, head_norm3, a_w_out, ctx_out):
    H = M_HEADS
    d = a_w_in.shape[1]
    dv = head_norm3.shape[-1] // H
    dk = dv // 2
    n_proj = 2 * H * dk + 2 * H * dv
    w_gates = _pad_cols(a_w_in[j, :, n_proj:], LANES)
    w_in = a_w_in[j, :, :n_proj].astype(BF16)
    w_out = a_w_out[j].astype(BF16)
    bias_row = _pad_cols(gate_bias.reshape(1, 4 * H), LANES)
    bias_col = bias_row.reshape(LANES, 1)

    projs = []
    for xs, row in streams:
        proj, gates = _fused_matmul(xs, w_in, n_out=n_proj, prologue="mod", norm=norm, mods=mods, row=row,
                                    extra_w=w_gates, out_dtype=BF16, name="mlstm_in_proj")
        projs.append((proj, gates, gates.T))

    zero = (jnp.zeros((H, dk, dv), F32), jnp.zeros((H, 1, dk), F32), jnp.zeros((H, 1, LANES), F32))
    (px, gx, gxt), (pc, gc, gct) = projs
    hn = (head_norm3, j)
    hcf, st_f = _mlstm_scan(pc, gc, gct, bias_row, bias_col, zero, reverse=False)
    hxf, _ = _mlstm_scan(px, gx, gxt, bias_row, bias_col, st_f, reverse=False)
    yc_pre, st_b = _mlstm_scan(pc, gc, gct, bias_row, bias_col, zero, reverse=True, finalize=(hcf, hn))
    yx_pre, _ = _mlstm_scan(px, gx, gxt, bias_row, bias_col, st_b, reverse=True, finalize=(hxf, hn))

    outs = []
    for (xs, row), y_pre, live in zip(streams, (yx_pre, yc_pre), (True, ctx_out)):
        if live:
            outs.append(_fused_matmul(y_pre, w_out, n_out=d, epilogue="residual", res=xs, mods=mods,
                                      row=row, name="mlstm_out_proj"))
        else:
            outs.append(xs)
    return outs


def _conv_layer(streams, mods, norm, b_w_in, b_w_dw, ln, b_w_out, j, live_flags, seq_lens):
    d = b_w_in.shape[1]
    w_in = b_w_in[j].astype(BF16)
    w_out = b_w_out[j].astype(BF16)
    outs = []
    for (xs, row), live, seq_len in zip(streams, live_flags, seq_lens):
        if not live:
            outs.append(xs)
            continue
        hglu = _fused_matmul(xs, w_in, n_out=d, prologue="mod", norm=norm, mods=mods, row=row,
                             epilogue="glu", w2_col_offset=d, name="conv_in_glu")
        hc = _dwconv(hglu, b_w_dw, j, seq_len=seq_len)
        outs.append(_fused_matmul(hc, w_out, n_out=d, prologue="ln_silu", ln=(ln, j),
                                  epilogue="residual", res=xs, mods=mods, row=row, name="conv_out_proj"))
    return outs


def kernel(x, c, ctx, c_ctx, ada_down, ada_up, ada_bias, norm_mix, norm_ffn, a_w_in, a_gate_bias, a_head_norm, a_w_out, b_w_in, b_w_dw, b_ln_g, b_ln_b, b_w_out, w_router, router_bias, e_w_gate, e_w_up, e_w_down, s_w_gate, s_w_up, s_w_down, final_norm):
    bsz, t, d = x.shape
    assert bsz == 1 and c.shape[0] == 1 and ctx.shape[0] == 1
    depth = ada_down.shape[0]
    n_experts = w_router.shape[-1]
    t_ctx = ctx.shape[1]

    cond = jnp.zeros((SUBLANES, d), F32).at[0].set(c[0]).at[1].set(c_ctx)
    mods_all = _ada_all(cond, ada_down, ada_up, ada_bias)
    norm_mix3 = norm_mix.reshape(depth, 1, d)
    norm_ffn3 = norm_ffn.reshape(depth, 1, d)
    head_norm3 = a_head_norm.reshape(a_head_norm.shape[0], 1, -1)
    ln3 = (b_ln_g.reshape(-1, 1, d), b_ln_b.reshape(-1, 1, d))
    fn = final_norm.reshape(1, d)

    xs, cs = x[0], ctx[0]
    for i in range(depth):
        kind, j = i % 2, i // 2
        ctx_live = any(l % 2 == 0 for l in range(i + 1, depth))
        mods = (mods_all, i)
        norm = (norm_mix3, i)
        streams = [(xs, 0), (cs, 1)]
        if kind == 0:
            xs, cs = _mlstm_layer(streams, mods, norm, a_w_in, j, a_gate_bias[j], head_norm3, a_w_out, ctx_live)
        else:
            xs, cs = _conv_layer(streams, mods, norm, b_w_in, b_w_dw, ln3, b_w_out, j,
                                 (True, ctx_live), (GRID_W, t_ctx))

        wr = _pad_cols(w_router[i], LANES)
        rb = _pad_cols(router_bias[i].reshape(1, n_experts), LANES)
        last = i == depth - 1
        new = []
        for (s, row), live in zip(((xs, 0), (cs, 1)), (True, ctx_live)):
            if live:
                s = _moe_layer(s, i, row, mods_all, norm_ffn3, wr, rb, e_w_gate, e_w_up, e_w_down,
                               s_w_gate, s_w_up, s_w_down, fn if (last and row == 0) else None)
            new.append(s)
        xs, cs = new
    return xs[None]
```

```python
import functools

import jax
import jax.numpy as jnp
from jax import lax
from jax.experimental import pallas as pl
from jax.experimental.pallas import tpu as pltpu

F32 = jnp.float32
BF16 = jnp.bfloat16
I32 = jnp.int32
U32 = jnp.uint32

GRID_W = 64
M_HEADS = 8
M_CHUNK = 128
CONV_W = 31
CONV_HALF = CONV_W // 2
TOP_K = 6
ROUTED_SCALE = 2.5
N_MOD = 6
EPS = 1e-6
SH1, SC1, G1, SH2, SC2, G2 = range(N_MOD)

LANES = 128
SUBLANES = 8
VMEM_LIMIT = 56 * 1024 * 1024
HIGHEST = lax.Precision.HIGHEST


def _cparams(sem):
    return pltpu.CompilerParams(dimension_semantics=sem, vmem_limit_bytes=VMEM_LIMIT)


def _sigmoid(v):
    return 1.0 / (1.0 + jnp.exp(-v))


def _silu(v):
    return v * _sigmoid(v)


def _log_sigmoid(v):
    return jnp.minimum(v, 0.0) - jnp.log(1.0 + jnp.exp(-jnp.abs(v)))


def _dot_nt(a, b_t):
    return lax.dot_general(a, b_t, (((1,), (1,)), ((), ())), preferred_element_type=F32)


_HI16 = 0xFFFF0000


def _pack_bf16_pairs(u):
    half = u.shape[1] // 2
    bits = lax.bitcast_convert_type(u.astype(F32), U32)
    return (bits[:, half:] & jnp.uint32(_HI16)) | (bits[:, :half] >> 16)


def _unpack_bf16_pairs(p):
    lo = lax.bitcast_convert_type(p << 16, F32)
    hi = lax.bitcast_convert_type(p & jnp.uint32(_HI16), F32)
    return jnp.concatenate([lo, hi], axis=1).astype(BF16)


def _rmsnorm(x, g):
    return x * lax.rsqrt(jnp.mean(x * x, axis=-1, keepdims=True) + EPS) * g


def _ada_body(cond_ref, down_ref, up_ref, bias_ref, o_ref):
    s = _silu(cond_ref[...])
    t = jnp.dot(s, down_ref[...], precision=HIGHEST, preferred_element_type=F32)
    o_ref[...] = jnp.dot(t, up_ref[...], precision=HIGHEST, preferred_element_type=F32) + bias_ref[...]


def _ada_all(cond, ada_down, ada_up, ada_bias):
    n_layers, d, rank = ada_down.shape
    n_out = ada_up.shape[-1]
    tn = d
    assert n_out % tn == 0
    return pl.pallas_call(
        _ada_body,
        out_shape=jax.ShapeDtypeStruct((n_layers, SUBLANES, n_out), F32),
        grid=(n_layers, n_out // tn),
        in_specs=[
            pl.BlockSpec((SUBLANES, d), lambda l, j: (0, 0)),
            pl.BlockSpec((None, d, rank), lambda l, j: (l, 0, 0)),
            pl.BlockSpec((None, rank, tn), lambda l, j: (l, 0, j)),
            pl.BlockSpec((None, 1, tn), lambda l, j: (l, 0, j)),
        ],
        out_specs=pl.BlockSpec((None, SUBLANES, tn), lambda l, j: (l, 0, j)),
        compiler_params=_cparams(("parallel", "arbitrary")),
        name="ada_mod",
    )(cond, ada_down, ada_up, ada_bias.reshape(n_layers, 1, n_out))


_PRO_ROWS = 32


def _mm_body(*refs, prologue, epilogue, extra, tm, row, d_mod):
    refs = list(refs)
    a_ref = refs.pop(0)
    if prologue == "mod":
        g_ref, mods_ref = refs.pop(0), refs.pop(0)
    elif prologue == "ln_silu":
        g_ref, b_ref = refs.pop(0), refs.pop(0)
    w_ref = refs.pop(0)
    if epilogue == "glu":
        w2_ref = refs.pop(0)
    elif epilogue == "residual":
        res_ref, gate_ref = refs.pop(0), refs.pop(0)
    if extra:
        wx_ref = refs.pop(0)
    o_ref = refs.pop(0)
    if extra:
        ox_ref = refs.pop(0)

    if prologue == "none":
        u = a_ref[...]
    else:
        u_ref = refs.pop(0)

        @pl.when(pl.program_id(1) == 0)
        def _():
            def rows(r, carry):
                sl = pl.ds(pl.multiple_of(r * _PRO_ROWS, _PRO_ROWS), _PRO_ROWS)
                x = a_ref[sl, :]
                if prologue == "mod":
                    shift = mods_ref[row:row + 1, SH1 * d_mod:(SH1 + 1) * d_mod]
                    scale = mods_ref[row:row + 1, SC1 * d_mod:(SC1 + 1) * d_mod]
                    y = _rmsnorm(x, g_ref[...]) * (1.0 + scale) + shift
                else:
                    mu = jnp.mean(x, axis=-1, keepdims=True)
                    xc = x - mu
                    var = jnp.mean(xc * xc, axis=-1, keepdims=True)
                    y = _silu(xc * lax.rsqrt(var + EPS) * g_ref[...] + b_ref[...])
                u_ref[sl, :] = y.astype(BF16)
                return carry

            lax.fori_loop(0, tm // _PRO_ROWS, rows, 0)

        u = u_ref[...]

    if extra:
        @pl.when(pl.program_id(1) == 0)
        def _():
            ox_ref[...] = _dot_nt(u, wx_ref[...].astype(BF16))

    acc = jnp.dot(u, w_ref[...], preferred_element_type=F32)
    if epilogue == "glu":
        acc2 = jnp.dot(u, w2_ref[...], preferred_element_type=F32)
        acc = acc * _sigmoid(acc2)
    elif epilogue == "residual":
        acc = res_ref[...] + gate_ref[row:row + 1, :] * acc
    o_ref[...] = acc.astype(o_ref.dtype)


def _fused_matmul(a, w, *, n_out, prologue="none", norm=None, ln=None, mods=None, row=0,
                  epilogue="plain", res=None, w2_col_offset=0, extra_w=None,
                  out_dtype=F32, tm=1024, tn=512, name="mm"):
    m, k = a.shape
    tm = min(tm, m)
    tn = min(tn, n_out)
    assert m % tm == 0 and n_out % tn == 0 and w.shape[0] == k and w.dtype == BF16 and tm % _PRO_ROWS == 0
    d_mod = None
    a_mode = {} if prologue == "none" else dict(pipeline_mode=pl.Buffered(1))
    in_specs = [pl.BlockSpec((tm, k), lambda i, j: (i, 0), **a_mode)]
    args = [a]
    if prologue == "mod":
        norm_arr, norm_layer = norm
        mods_all, mods_layer = mods
        d_mod = mods_all.shape[-1] // N_MOD
        in_specs += [pl.BlockSpec((None, 1, k), lambda i, j: (norm_layer, 0, 0)),
                     pl.BlockSpec((None, SUBLANES, mods_all.shape[-1]), lambda i, j: (mods_layer, 0, 0))]
        args += [norm_arr, mods_all]
    elif prologue == "ln_silu":
        (ln_g, ln_b), ln_layer = ln
        in_specs += [pl.BlockSpec((None, 1, k), lambda i, j: (ln_layer, 0, 0))] * 2
        args += [ln_g, ln_b]
    in_specs.append(pl.BlockSpec((k, tn), lambda i, j: (0, j)))
    args.append(w)
    if epilogue == "glu":
        off = w2_col_offset // tn
        in_specs.append(pl.BlockSpec((k, tn), lambda i, j: (0, j + off)))
        args.append(w)
    elif epilogue == "residual":
        mods_all, mods_layer = mods
        goff = G1 * (mods_all.shape[-1] // N_MOD) // tn
        in_specs += [pl.BlockSpec((tm, tn), lambda i, j: (i, j)),
                     pl.BlockSpec((None, SUBLANES, tn), lambda i, j: (mods_layer, 0, goff + j))]
        args += [res, mods_all]
    out_shape = [jax.ShapeDtypeStruct((m, n_out), out_dtype)]
    out_specs = [pl.BlockSpec((tm, tn), lambda i, j: (i, j))]
    if extra_w is not None:
        in_specs.append(pl.BlockSpec((LANES, k), lambda i, j: (0, 0)))
        args.append(extra_w)
        out_shape.append(jax.ShapeDtypeStruct((m, LANES), F32))
        out_specs.append(pl.BlockSpec((tm, LANES), lambda i, j: (i, 0)))
    scratch = [] if prologue == "none" else [pltpu.VMEM((tm, k), BF16)]
    outs = pl.pallas_call(
        functools.partial(_mm_body, prologue=prologue, epilogue=epilogue, extra=extra_w is not None,
                          tm=tm, row=row, d_mod=d_mod),
        out_shape=out_shape,
        grid=(m // tm, n_out // tn),
        in_specs=in_specs,
        out_specs=out_specs,
        scratch_shapes=scratch,
        compiler_params=_cparams(("parallel", "arbitrary")),
        name=name,
    )(*args)
    return outs if extra_w is not None else outs[0]


def _scan_body(*refs, reverse, finalize, dk, dv):
    refs = list(refs)
    q_ref, k_ref, v_ref = refs.pop(0), refs.pop(0), refs.pop(0)
    gcol_ref, grow_ref, bias_row_ref, bias_col_ref = refs.pop(0), refs.pop(0), refs.pop(0), refs.pop(0)
    c0_ref, n0_ref, m0_ref = refs.pop(0), refs.pop(0), refs.pop(0)
    if finalize:
        o_ref, hprev_ref, hnorm_ref = refs.pop(0), refs.pop(0), refs.pop(0)
    out_ref, c_ref, n_ref, m_ref = refs.pop(0), refs.pop(0), refs.pop(0), refs.pop(0)

    L = M_CHUNK
    H = M_HEADS

    @pl.when(pl.program_id(0) == 0)
    def _():
        c_ref[...] = c0_ref[...]
        n_ref[...] = n0_ref[...]
        m_ref[...] = m0_ref[...]

    kind_i = 2 if reverse else 0
    kind_f = kind_i + 1

    t_idx = lax.broadcasted_iota(I32, (L, L), 0)
    s_idx = lax.broadcasted_iota(I32, (L, L), 1)
    mask = (s_idx >= t_idx) if reverse else (s_idx <= t_idx)
    mask_f = mask.astype(F32)
    mask_t = ((t_idx >= s_idx) if reverse else (t_idx <= s_idx)).astype(F32)
    q_scale = dk ** -0.5

    gcol = gcol_ref[...] + bias_row_ref[...]
    grow = grow_ref[...] + bias_col_ref[...]
    lf_col = _log_sigmoid(gcol)
    lf_row = _log_sigmoid(grow[kind_f * H:(kind_f + 1) * H, :])
    b_col_all = jnp.dot(mask_f, lf_col, precision=HIGHEST, preferred_element_type=F32)
    b_row_all = jnp.dot(lf_row, mask_t, precision=HIGHEST, preferred_element_type=F32)

    for h in range(H):
        ci, cf = kind_i * H + h, kind_f * H + h
        b_col = b_col_all[:, cf:cf + 1]
        b_row = b_row_all[h:h + 1, :]
        i_col = gcol[:, ci:ci + 1]
        i_row = grow[ci:ci + 1, :]
        m_prev = m_ref[h][:, :1]
        q = q_ref[:, h * dk:(h + 1) * dk]
        k = k_ref[:, h * dk:(h + 1) * dk]
        v = v_ref[:, h * dv:(h + 1) * dv]
        c_state = c_ref[h]
        n_state = n_ref[h]

        dmat = jnp.where(mask, b_col - b_row + i_row, -jnp.inf)
        m_inter = b_col + m_prev
        m_t = jnp.maximum(m_inter, jnp.max(dmat, axis=-1, keepdims=True))
        w_intra = jnp.exp(dmat - m_t) * q_scale
        w_inter = jnp.exp(m_inter - m_t) * q_scale
        s = lax.dot_general(q, k, (((1,), (1,)), ((), ())), preferred_element_type=F32) * w_intra
        num = jnp.dot(s.astype(BF16), v, preferred_element_type=F32)
        num = num + w_inter * jnp.dot(q, c_state.astype(BF16), preferred_element_type=F32)
        qn = jnp.sum(q.astype(F32) * n_state, axis=-1, keepdims=True)
        den = jnp.sum(s, axis=-1, keepdims=True) + w_inter * qn
        hv = num * (1.0 / jnp.maximum(jnp.abs(den), jnp.exp(-m_t)))

        b_last = jnp.sum(lf_row[h:h + 1, :], axis=-1, keepdims=True)
        m_new = jnp.maximum(b_last + m_prev, jnp.max(b_last - b_row + i_row, axis=-1, keepdims=True))
        kw = k.astype(F32) * jnp.exp(b_last - b_col + i_col - m_new)
        decay = jnp.exp(b_last + m_prev - m_new)
        c_ref[h] = decay * c_state + lax.dot_general(
            kw.astype(BF16), v, (((0,), (0,)), ((), ())), preferred_element_type=F32)
        n_ref[h] = decay * n_state + jnp.sum(kw, axis=0, keepdims=True)
        m_ref[h] = jnp.broadcast_to(m_new, (1, LANES))

        cols = slice(h * dv, (h + 1) * dv)
        if finalize:
            ht = hv + hprev_ref[:, cols]
            ht = ht * lax.rsqrt(jnp.mean(ht * ht, axis=-1, keepdims=True) + EPS)
            ht = ht * hnorm_ref[:, cols] * _sigmoid(o_ref[:, cols].astype(F32))
            out_ref[:, cols] = ht.astype(out_ref.dtype)
        else:
            out_ref[:, cols] = hv


def _mlstm_scan(proj, gcol, grow, bias_row, bias_col, state, *, reverse, finalize=None):
    t = proj.shape[0]
    H, L = M_HEADS, M_CHUNK
    c0, n0, m0 = state
    dk, dv = c0.shape[1], c0.shape[2]
    nc = t // L
    cidx = (lambda i: nc - 1 - i) if reverse else (lambda i: i)
    qk_w, v_w = H * dk, H * dv
    assert 2 * qk_w == v_w
    full = lambda shape: pl.BlockSpec(shape, lambda i: (0,) * len(shape))
    in_specs = [
        pl.BlockSpec((L, qk_w), lambda i: (cidx(i), 0)),
        pl.BlockSpec((L, qk_w), lambda i: (cidx(i), 1)),
        pl.BlockSpec((L, v_w), lambda i: (cidx(i), 1)),
        pl.BlockSpec((L, LANES), lambda i: (cidx(i), 0)),
        pl.BlockSpec((LANES, L), lambda i: (0, cidx(i))),
        full((1, LANES)), full((LANES, 1)),
        full(c0.shape), full(n0.shape), full(m0.shape),
    ]
    args = [proj, proj, proj, gcol, grow, bias_row, bias_col, c0, n0, m0]
    if finalize is not None:
        hprev, (hnorm, hn_layer) = finalize
        in_specs += [pl.BlockSpec((L, v_w), lambda i: (cidx(i), 2)),
                     pl.BlockSpec((L, v_w), lambda i: (cidx(i), 0)),
                     pl.BlockSpec((None, 1, v_w), lambda i: (hn_layer, 0, 0))]
        args += [proj, hprev, hnorm]
    out_dtype = BF16 if finalize is not None else F32
    out, c1, n1, m1 = pl.pallas_call(
        functools.partial(_scan_body, reverse=reverse, finalize=finalize is not None, dk=dk, dv=dv),
        out_shape=(jax.ShapeDtypeStruct((t, v_w), out_dtype),
                   jax.ShapeDtypeStruct(c0.shape, F32),
                   jax.ShapeDtypeStruct(n0.shape, F32),
                   jax.ShapeDtypeStruct(m0.shape, F32)),
        grid=(nc,),
        in_specs=in_specs,
        out_specs=(pl.BlockSpec((L, v_w), lambda i: (cidx(i), 0)),
                   full(c0.shape), full(n0.shape), full(m0.shape)),
        compiler_params=_cparams(("arbitrary",)),
        name="mlstm_scan_bwd" if reverse else "mlstm_scan_fwd",
    )(*args)
    return out, (c1, n1, m1)


_CONV_LANES = 512
_CONV_ROWS = 64
_CONV_PAD = 16


def _conv_body(x_ref, w_ref, o_ref, pad_ref, shift_ref, *, seq_len, n_seq):
    zeros = jnp.zeros((_CONV_PAD, _CONV_LANES), F32)
    pad_ref[0:_CONV_PAD, :] = zeros
    pad_ref[_CONV_PAD + seq_len:2 * _CONV_PAD + seq_len, :] = zeros
    n_shift_rows = shift_ref.shape[1]

    def one_seq(s, carry):
        base = pl.multiple_of(s * seq_len, seq_len)
        pad_ref[_CONV_PAD:_CONV_PAD + seq_len, :] = x_ref[pl.ds(base, seq_len), :]
        for b in range(1, SUBLANES):
            shift_ref[b] = pad_ref[b:b + n_shift_rows, :]
        for r0 in range(0, seq_len, _CONV_ROWS):
            acc = jnp.zeros((_CONV_ROWS, _CONV_LANES), F32)
            for j in range(CONV_W):
                start = _CONV_PAD - CONV_HALF + j + r0
                b, a = start % SUBLANES, start - start % SUBLANES
                src = pad_ref if b == 0 else shift_ref.at[b]
                acc = acc + w_ref[j:j + 1, :] * src[a:a + _CONV_ROWS, :]
            o_ref[pl.ds(base + r0, _CONV_ROWS), :] = acc
        return carry

    lax.fori_loop(0, n_seq, one_seq, 0)


def _dwconv(x, w_dw, layer, *, seq_len):
    t, d = x.shape
    n_seq = max(1, min(512, t) // seq_len)
    rows = n_seq * seq_len
    assert t % rows == 0 and d % _CONV_LANES == 0 and seq_len % _CONV_ROWS == 0
    return pl.pallas_call(
        functools.partial(_conv_body, seq_len=seq_len, n_seq=n_seq),
        out_shape=jax.ShapeDtypeStruct((t, d), F32),
        grid=(t // rows, d // _CONV_LANES),
        in_specs=[pl.BlockSpec((rows, _CONV_LANES), lambda i, j: (i, j)),
                  pl.BlockSpec((None, CONV_W, _CONV_LANES), lambda i, j: (layer, 0, j))],
        out_specs=pl.BlockSpec((rows, _CONV_LANES), lambda i, j: (i, j)),
        scratch_shapes=[pltpu.VMEM((seq_len + 2 * _CONV_PAD, _CONV_LANES), F32),
                        pltpu.VMEM((SUBLANES, seq_len + 2 * _CONV_PAD - SUBLANES, _CONV_LANES), F32)],
        compiler_params=_cparams(("parallel", "parallel")),
        name="dwconv",
    )(x, w_dw)


META_IDX, META_RANK, META_W = 0, 8, 16


def _router_body(x_ref, g_ref, mods_ref, wr_ref, rb_ref, swg_ref, swu_ref, swd_ref,
                 u_ref, xb_ref, meta_ref, counts_ref, carry_ref, *, n_experts, row, d, tm):
    @pl.when(pl.program_id(0) == 0)
    def _():
        carry_ref[...] = jnp.zeros_like(carry_ref)

    x = x_ref[...]
    shift = mods_ref[row:row + 1, SH2 * d:(SH2 + 1) * d]
    scale = mods_ref[row:row + 1, SC2 * d:(SC2 + 1) * d]
    gate2 = mods_ref[row:row + 1, G2 * d:(G2 + 1) * d]
    uf = _rmsnorm(x, g_ref[...]) * (1.0 + scale) + shift
    u = uf.astype(BF16)
    u_ref[...] = _pack_bf16_pairs(u)

    hs = _silu(_dot_nt(u, swg_ref[...].astype(BF16)))
    hs = hs * _dot_nt(u, swu_ref[...].astype(BF16))
    shared = jnp.dot(hs.astype(BF16), swd_ref[...].astype(BF16), preferred_element_type=F32)
    xb_ref[...] = x + gate2 * shared

    logits = _dot_nt(u, wr_ref[...].astype(BF16))
    scores = _sigmoid(logits)
    lane = lax.broadcasted_iota(I32, scores.shape, 1)
    sel = jnp.where(lane < n_experts, scores + rb_ref[...], -jnp.inf)
    picked = jnp.zeros_like(scores)
    chosen = jnp.zeros_like(scores)
    firsts = []
    for _ in range(TOP_K):
        mx = jnp.max(sel, axis=-1, keepdims=True)
        first = jnp.min(jnp.where(sel == mx, lane, LANES), axis=-1, keepdims=True)
        hit = lane == first
        picked = jnp.where(hit, scores, picked)
        chosen = jnp.where(hit, 1.0, chosen)
        sel = jnp.where(hit, -jnp.inf, sel)
        firsts.append(first)
    gates = picked / jnp.sum(picked, axis=-1, keepdims=True) * ROUTED_SCALE

    r_idx = lax.broadcasted_iota(I32, (tm, tm), 0)
    c_idx = lax.broadcasted_iota(I32, (tm, tm), 1)
    before = (c_idx < r_idx).astype(BF16)
    pos = jnp.dot(before, chosen.astype(BF16), preferred_element_type=F32) + carry_ref[...]
    meta = jnp.zeros_like(scores)
    for kk, first in enumerate(firsts):
        hit = lane == first
        rank = jnp.sum(jnp.where(hit, pos, 0.0), axis=-1, keepdims=True)
        wk = jnp.sum(jnp.where(hit, gates, 0.0), axis=-1, keepdims=True)
        meta = jnp.where(lane == META_IDX + kk, first.astype(F32), meta)
        meta = jnp.where(lane == META_RANK + kk, rank, meta)
        meta = jnp.where(lane == META_W + kk, wk, meta)
    meta_ref[...] = meta
    carry_ref[...] += jnp.sum(chosen, axis=0, keepdims=True)
    counts_ref[...] = carry_ref[...]


def _router(x, norm_ffn3, layer, mods_all, row, wr_pad, rb_pad, s_w_gate_t, s_w_up_t, s_w_down, n_experts, tm=256):
    t, d = x.shape
    tm = min(tm, t)
    ds_ = s_w_gate_t.shape[1]
    c2 = lambda i: (0, 0)
    lay3 = lambda i: (layer, 0, 0)
    blk = pl.BlockSpec((tm, d), lambda i: (i, 0))
    return pl.pallas_call(
        functools.partial(_router_body, n_experts=n_experts, row=row, d=d, tm=tm),
        out_shape=(jax.ShapeDtypeStruct((t, d // 2), U32), jax.ShapeDtypeStruct((t, d), F32),
                   jax.ShapeDtypeStruct((t, LANES), F32), jax.ShapeDtypeStruct((1, LANES), F32)),
        grid=(t // tm,),
        in_specs=[blk,
                  pl.BlockSpec((None, 1, d), lay3),
                  pl.BlockSpec((None, SUBLANES, mods_all.shape[-1]), lay3),
                  pl.BlockSpec((LANES, d), c2), pl.BlockSpec((1, LANES), c2),
                  pl.BlockSpec((None, ds_, d), lay3), pl.BlockSpec((None, ds_, d), lay3),
                  pl.BlockSpec((None, ds_, d), lay3)],
        out_specs=(pl.BlockSpec((tm, d // 2), lambda i: (i, 0)), blk,
                   pl.BlockSpec((tm, LANES), lambda i: (i, 0)), pl.BlockSpec((1, LANES), c2)),
        scratch_shapes=[pltpu.VMEM((1, LANES), F32)],
        compiler_params=_cparams(("arbitrary",)),
        name="moe_router",
    )(x, norm_ffn3, mods_all, wr_pad, rb_pad, s_w_gate_t, s_w_up_t, s_w_down)


def _slot(starts_ref, idx_ref, rank_ref, a):
    return starts_ref[idx_ref[0, a]] + rank_ref[0, a]


def _dispatch_body(starts_ref, idx_ref, rank_ref, u_ref, xs_hbm, sem, *, tokens):
    def issue(t, carry):
        for kk in range(TOP_K):
            s = _slot(starts_ref, idx_ref, rank_ref, t * TOP_K + kk)
            pltpu.make_async_copy(u_ref.at[pl.ds(t, 1), :], xs_hbm.at[pl.ds(s, 1), :], sem).start()
        return carry

    lax.fori_loop(0, tokens, issue, 0)
    for _ in range(TOP_K):
        pltpu.make_async_copy(u_ref, xs_hbm.at[pl.ds(0, tokens), :], sem).wait()


def _dispatch(u, starts_ext, idx3, rank3, tokens):
    t, d = u.shape
    smem_blk = pl.BlockSpec((None, 1, tokens * TOP_K), lambda i, st: (i, 0, 0), memory_space=pltpu.SMEM)
    grid_spec = pltpu.PrefetchScalarGridSpec(
        num_scalar_prefetch=1,
        grid=(t // tokens,),
        in_specs=[smem_blk, smem_blk, pl.BlockSpec((tokens, d), lambda i, st: (i, 0))],
        out_specs=pl.BlockSpec(memory_space=pl.ANY),
        scratch_shapes=[pltpu.SemaphoreType.DMA],
    )
    return pl.pallas_call(
        functools.partial(_dispatch_body, tokens=tokens),
        out_shape=jax.ShapeDtypeStruct((t * TOP_K, d), u.dtype),
        grid_spec=grid_spec,
        compiler_params=_cparams(("arbitrary",)),
        name="moe_dispatch",
    )(starts_ext, idx3, rank3, u)


_FFN_ROWS = 256


def _ffn_body(vt_ref, ve_ref, starts_ref, nvis_ref, xs_ref, wg_ref, wu_ref, wd_ref, ys_ref):
    v = pl.program_id(0)

    @pl.when(v < nvis_ref[0])
    def _():
        tile = vt_ref[v]
        e = ve_ref[v]
        x = _unpack_bf16_pairs(xs_ref[...])
        hg = _dot_nt(x, wg_ref[...].astype(BF16))
        hu = _dot_nt(x, wu_ref[...].astype(BF16))
        hh = (_silu(hg) * hu).astype(BF16)
        y = jnp.dot(hh, wd_ref[...].astype(BF16), preferred_element_type=F32)
        first = jnp.logical_or(v == 0, vt_ref[jnp.maximum(v - 1, 0)] != tile)

        @pl.when(first)
        def _():
            ys_ref[...] = y

        @pl.when(jnp.logical_not(first))
        def _():
            r = lax.broadcasted_iota(I32, (_FFN_ROWS, 1), 0) + tile * _FFN_ROWS
            mine = jnp.logical_and(r >= starts_ref[e], r < starts_ref[e + 1])
            ys_ref[...] = jnp.where(mine, y, ys_ref[...])


def _grouped_ffn(xs, vt, ve, starts, nvis, e_w_gate_t, e_w_up_t, e_w_down, layer):
    ns = xs.shape[0]
    di, d = e_w_down.shape[2:]
    nv = vt.shape[0]
    w_blk = pl.BlockSpec((None, None, di, d), lambda v, vt, ve, st, nn: (layer, ve[v], 0, 0))
    grid_spec = pltpu.PrefetchScalarGridSpec(
        num_scalar_prefetch=4,
        grid=(nv,),
        in_specs=[pl.BlockSpec((_FFN_ROWS, d // 2), lambda v, vt, ve, st, nn: (vt[v], 0)), w_blk, w_blk, w_blk],
        out_specs=pl.BlockSpec((_FFN_ROWS, d), lambda v, vt, ve, st, nn: (vt[v], 0)),
    )
    return pl.pallas_call(
        _ffn_body,
        out_shape=jax.ShapeDtypeStruct((ns, d), F32),
        grid_spec=grid_spec,
        compiler_params=_cparams(("arbitrary",)),
        name="moe_grouped_ffn",
    )(vt, ve, starts, nvis, xs, e_w_gate_t, e_w_up_t, e_w_down)


_COMB_TOKENS = 128


def _combine_body(*refs, row, d, final):
    refs = list(refs)
    starts_ref, idx_ref, rank_ref = refs.pop(0), refs.pop(0), refs.pop(0)
    meta_ref, xb_ref, mods_ref = refs.pop(0), refs.pop(0), refs.pop(0)
    fn_ref = refs.pop(0) if final else None
    ys_hbm, o_ref, gbuf, sem = refs

    def issue(t, carry):
        for kk in range(TOP_K):
            s = _slot(starts_ref, idx_ref, rank_ref, t * TOP_K + kk)
            pltpu.make_async_copy(ys_hbm.at[pl.ds(s, 1), :], gbuf.at[kk, pl.ds(t, 1), :], sem).start()
        return carry

    lax.fori_loop(0, _COMB_TOKENS, issue, 0)
    for kk in range(TOP_K):
        pltpu.make_async_copy(ys_hbm.at[pl.ds(0, _COMB_TOKENS), :], gbuf.at[kk], sem).wait()

    meta = meta_ref[...]
    acc = meta[:, META_W:META_W + 1] * gbuf[0]
    for kk in range(1, TOP_K):
        acc = acc + meta[:, META_W + kk:META_W + kk + 1] * gbuf[kk]
    out = xb_ref[...] + mods_ref[row:row + 1, G2 * d:(G2 + 1) * d] * acc
    if final:
        out = _rmsnorm(out, fn_ref[...])
    o_ref[...] = out


def _combine(ys, starts_ext, idx3, rank3, meta, xb, mods_all, layer, row, final_norm=None):
    t, d = xb.shape
    tc = _COMB_TOKENS
    assert t % tc == 0
    blk = pl.BlockSpec((tc, d), lambda i, st: (i, 0))
    smem_blk = pl.BlockSpec((None, 1, tc * TOP_K), lambda i, st: (i, 0, 0), memory_space=pltpu.SMEM)
    in_specs = [smem_blk, smem_blk, pl.BlockSpec((tc, LANES), lambda i, st: (i, 0)), blk,
                pl.BlockSpec((None, SUBLANES, mods_all.shape[-1]), lambda i, st: (layer, 0, 0))]
    args = [idx3, rank3, meta, xb, mods_all]
    if final_norm is not None:
        in_specs.append(pl.BlockSpec((1, d), lambda i, st: (0, 0)))
        args.append(final_norm)
    in_specs.append(pl.BlockSpec(memory_space=pl.ANY))
    args.append(ys)
    grid_spec = pltpu.PrefetchScalarGridSpec(
        num_scalar_prefetch=1,
        grid=(t // tc,),
        in_specs=in_specs,
        out_specs=blk,
        scratch_shapes=[pltpu.VMEM((TOP_K, tc, d), F32), pltpu.SemaphoreType.DMA],
    )
    return pl.pallas_call(
        functools.partial(_combine_body, row=row, d=d, final=final_norm is not None),
        out_shape=jax.ShapeDtypeStruct((t, d), F32),
        grid_spec=grid_spec,
        compiler_params=_cparams(("arbitrary",)),
        name="moe_combine",
    )(starts_ext, *args)


def _moe_layer(x, layer, row, mods_all, norm_ffn3, wr_pad, rb_pad, e_w_gate_t, e_w_up_t, e_w_down,
               s_w_gate_t, s_w_up_t, s_w_down, final_norm):
    t, d = x.shape
    n_experts = e_w_down.shape[1]
    u, xb, meta, counts = _router(x, norm_ffn3, layer, mods_all, row, wr_pad, rb_pad,
                                  s_w_gate_t, s_w_up_t, s_w_down, n_experts)
    counts = counts[0, :n_experts].astype(I32)
    ends = jnp.cumsum(counts)
    starts = ends - counts
    idx = meta[:, META_IDX:META_IDX + TOP_K].astype(I32).reshape(-1)
    rank = meta[:, META_RANK:META_RANK + TOP_K].astype(I32).reshape(-1)
    ns = t * TOP_K
    n_tiles = ns // _FFN_ROWS
    first_tile = starts // _FFN_ROWS
    ntile_e = jnp.where(counts > 0, (ends - 1) // _FFN_ROWS - first_tile + 1, 0)
    vend = jnp.cumsum(ntile_e)
    vstart = vend - ntile_e
    nvis = vend[-1]
    nv = n_tiles + n_experts - 1
    v = jnp.minimum(jnp.arange(nv, dtype=I32), nvis - 1)
    ve = jnp.minimum(jnp.sum((vend[None, :] <= v[:, None]).astype(I32), axis=1), n_experts - 1)
    onehot = (ve[:, None] == jnp.arange(n_experts, dtype=I32)[None, :]).astype(I32)
    vt = (jnp.sum(onehot * (first_tile - vstart)[None, :], axis=1) + v).astype(I32)
    starts_ext = jnp.concatenate([starts, ends[-1:]]).astype(I32)

    disp_tokens = min(256, t)
    per = lambda a, n: a.reshape(t // n, 1, n * TOP_K)
    xs = _dispatch(u, starts_ext, per(idx, disp_tokens), per(rank, disp_tokens), disp_tokens)
    ys = _grouped_ffn(xs, vt, ve, starts_ext, nvis.reshape(1).astype(I32), e_w_gate_t, e_w_up_t, e_w_down, layer)
    return _combine(ys, starts_ext, per(idx, _COMB_TOKENS), per(rank, _COMB_TOKENS), meta, xb,
                    mods_all, layer, row, final_norm)


def _pad_cols(a, n):
    return jnp.pad(a, ((0, 0), (0, n - a.shape[1])))


def _mlstm_layer(streams, mods, norm, a_w_in, j, gate_bias, head_norm3, a_w_out, ctx_out):
    H = M_HEADS
    d = a_w_in.shape[1]
    dv = head_norm3.shape[-1] // H
    dk = dv // 2
    n_proj = 2 * H * dk + 2 * H * dv
    a_w_in_t = jnp.swapaxes(a_w_in, 1, 2)[j]
    w_gates = jnp.pad(a_w_in_t[n_proj:], ((0, LANES - 4 * H), (0, 0)))
    w_in = a_w_in_t[:n_proj].astype(BF16).T
    w_out = a_w_out[j].astype(BF16)
    bias_row = _pad_cols(gate_bias.reshape(1, 4 * H), LANES)
    bias_col = bias_row.reshape(LANES, 1)

    projs = []
    for xs, row in streams:
        proj, gates = _fused_matmul(xs, w_in, n_out=n_proj, prologue="mod", norm=norm, mods=mods, row=row,
                                    extra_w=w_gates, out_dtype=BF16, name="mlstm_in_proj")
        projs.append((proj, gates, gates.T))

    zero = (jnp.zeros((H, dk, dv), F32), jnp.zeros((H, 1, dk), F32), jnp.zeros((H, 1, LANES), F32))
    (px, gx, gxt), (pc, gc, gct) = projs
    hn = (head_norm3, j)
    hcf, st_f = _mlstm_scan(pc, gc, gct, bias_row, bias_col, zero, reverse=False)
    hxf, _ = _mlstm_scan(px, gx, gxt, bias_row, bias_col, st_f, reverse=False)
    yc_pre, st_b = _mlstm_scan(pc, gc, gct, bias_row, bias_col, zero, reverse=True, finalize=(hcf, hn))
    yx_pre, _ = _mlstm_scan(px, gx, gxt, bias_row, bias_col, st_b, reverse=True, finalize=(hxf, hn))

    outs = []
    for (xs, row), y_pre, live in zip(streams, (yx_pre, yc_pre), (True, ctx_out)):
        if live:
            outs.append(_fused_matmul(y_pre, w_out, n_out=d, epilogue="residual", res=xs, mods=mods,
                                      row=row, name="mlstm_out_proj"))
        else:
            outs.append(xs)
    return outs


def _conv_layer(streams, mods, norm, b_w_in, b_w_dw, ln, b_w_out, j, live_flags, seq_lens):
    d = b_w_in.shape[1]
    w_in = b_w_in[j].astype(BF16)
    w_out = b_w_out[j].astype(BF16)
    outs = []
    for (xs, row), live, seq_len in zip(streams, live_flags, seq_lens):
        if not live:
            outs.append(xs)
            continue
        hglu = _fused_matmul(xs, w_in, n_out=d, prologue="mod", norm=norm, mods=mods, row=row,
                             epilogue="glu", w2_col_offset=d, name="conv_in_glu")
        hc = _dwconv(hglu, b_w_dw, j, seq_len=seq_len)
        outs.append(_fused_matmul(hc, w_out, n_out=d, prologue="ln_silu", ln=(ln, j),
                                  epilogue="residual", res=xs, mods=mods, row=row, name="conv_out_proj"))
    return outs


def kernel(x, c, ctx, c_ctx, ada_down, ada_up, ada_bias, norm_mix, norm_ffn, a_w_in, a_gate_bias, a_head_norm, a_w_out, b_w_in, b_w_dw, b_ln_g, b_ln_b, b_w_out, w_router, router_bias, e_w_gate, e_w_up, e_w_down, s_w_gate, s_w_up, s_w_down, final_norm):
    bsz, t, d = x.shape
    assert bsz == 1 and c.shape[0] == 1 and ctx.shape[0] == 1
    depth = ada_down.shape[0]
    n_experts = w_router.shape[-1]
    t_ctx = ctx.shape[1]

    cond = jnp.zeros((SUBLANES, d), F32).at[0].set(c[0]).at[1].set(c_ctx)
    mods_all = _ada_all(cond, ada_down, ada_up, ada_bias)
    norm_mix3 = norm_mix.reshape(depth, 1, d)
    norm_ffn3 = norm_ffn.reshape(depth, 1, d)
    head_norm3 = a_head_norm.reshape(a_head_norm.shape[0], 1, -1)
    ln3 = (b_ln_g.reshape(-1, 1, d), b_ln_b.reshape(-1, 1, d))
    fn = final_norm.reshape(1, d)
    e_gate_t, e_up_t = jnp.swapaxes(e_w_gate, 2, 3), jnp.swapaxes(e_w_up, 2, 3)
    s_gate_t, s_up_t = jnp.swapaxes(s_w_gate, 1, 2), jnp.swapaxes(s_w_up, 1, 2)

    xs, cs = x[0], ctx[0]
    for i in range(depth):
        kind, j = i % 2, i // 2
        ctx_live = any(l % 2 == 0 for l in range(i + 1, depth))
        mods = (mods_all, i)
        norm = (norm_mix3, i)
        streams = [(xs, 0), (cs, 1)]
        if kind == 0:
            xs, cs = _mlstm_layer(streams, mods, norm, a_w_in, j, a_gate_bias[j], head_norm3, a_w_out, ctx_live)
        else:
            xs, cs = _conv_layer(streams, mods, norm, b_w_in, b_w_dw, ln3, b_w_out, j,
                                 (True, ctx_live), (GRID_W, t_ctx))

        wr = jnp.pad(w_router[i].T, ((0, LANES - n_experts), (0, 0)))
        rb = _pad_cols(router_bias[i].reshape(1, n_experts), LANES)
        last = i == depth - 1
        new = []
        for (s, row), live in zip(((xs, 0), (cs, 1)), (True, ctx_live)):
            if live:
                s = _moe_layer(s, i, row, mods_all, norm_ffn3, wr, rb, e_gate_t, e_up_t, e_w_down,
                               s_gate_t, s_up_t, s_w_down, fn if (last and row == 0) else None)
            new.append(s)
        xs, cs = new
    return xs[None]
```

```python
import functools

import jax
import jax.numpy as jnp
from jax import lax
from jax.experimental import pallas as pl
from jax.experimental.pallas import tpu as pltpu

F32 = jnp.float32
BF16 = jnp.bfloat16
I32 = jnp.int32
U32 = jnp.uint32

GRID_W = 64
M_HEADS = 8
M_CHUNK = 128
CONV_W = 31
CONV_HALF = CONV_W // 2
TOP_K = 6
ROUTED_SCALE = 2.5
N_MOD = 6
EPS = 1e-6
SH1, SC1, G1, SH2, SC2, G2 = range(N_MOD)

LANES = 128
SUBLANES = 8
VMEM_LIMIT = 56 * 1024 * 1024
HIGHEST = lax.Precision.HIGHEST


def _cparams(sem):
    return pltpu.CompilerParams(dimension_semantics=sem, vmem_limit_bytes=VMEM_LIMIT)


def _sigmoid(v):
    return 1.0 / (1.0 + jnp.exp(-v))


def _silu(v):
    return v * _sigmoid(v)


def _log_sigmoid(v):
    return jnp.minimum(v, 0.0) - jnp.log(1.0 + jnp.exp(-jnp.abs(v)))


def _dot_nt(a, b_t):
    return lax.dot_general(a, b_t, (((1,), (1,)), ((), ())), preferred_element_type=F32)


_HI16 = 0xFFFF0000


def _pack_bf16_pairs(u):
    half = u.shape[1] // 2
    bits = lax.bitcast_convert_type(u.astype(F32), U32)
    return (bits[:, half:] & jnp.uint32(_HI16)) | (bits[:, :half] >> 16)


def _unpack_bf16_pairs(p):
    lo = lax.bitcast_convert_type(p << 16, F32)
    hi = lax.bitcast_convert_type(p & jnp.uint32(_HI16), F32)
    return jnp.concatenate([lo, hi], axis=1).astype(BF16)


def _rmsnorm(x, g):
    return x * lax.rsqrt(jnp.mean(x * x, axis=-1, keepdims=True) + EPS) * g


def _ada_body(cond_ref, down_ref, up_ref, bias_ref, o_ref):
    s = _silu(cond_ref[...])
    t = jnp.dot(s, down_ref[...], precision=HIGHEST, preferred_element_type=F32)
    o_ref[...] = jnp.dot(t, up_ref[...], precision=HIGHEST, preferred_element_type=F32) + bias_ref[...]


def _ada_all(cond, ada_down, ada_up, ada_bias):
    n_layers, d, rank = ada_down.shape
    n_out = ada_up.shape[-1]
    tn = d
    assert n_out % tn == 0
    return pl.pallas_call(
        _ada_body,
        out_shape=jax.ShapeDtypeStruct((n_layers, SUBLANES, n_out), F32),
        grid=(n_layers, n_out // tn),
        in_specs=[
            pl.BlockSpec((SUBLANES, d), lambda l, j: (0, 0)),
            pl.BlockSpec((None, d, rank), lambda l, j: (l, 0, 0)),
            pl.BlockSpec((None, rank, tn), lambda l, j: (l, 0, j)),
            pl.BlockSpec((None, 1, tn), lambda l, j: (l, 0, j)),
        ],
        out_specs=pl.BlockSpec((None, SUBLANES, tn), lambda l, j: (l, 0, j)),
        compiler_params=_cparams(("parallel", "arbitrary")),
        name="ada_mod",
    )(cond, ada_down, ada_up, ada_bias.reshape(n_layers, 1, n_out))


_PRO_ROWS = 32


def _mm_body(*refs, prologue, epilogue, extra, tm, row, d_mod, w_transposed):
    refs = list(refs)
    a_ref = refs.pop(0)
    if prologue == "mod":
        g_ref, mods_ref = refs.pop(0), refs.pop(0)
    elif prologue == "ln_silu":
        g_ref, b_ref = refs.pop(0), refs.pop(0)
    w_ref = refs.pop(0)
    if epilogue == "glu":
        w2_ref = refs.pop(0)
    elif epilogue == "residual":
        res_ref, gate_ref = refs.pop(0), refs.pop(0)
    if extra:
        wx_ref = refs.pop(0)
    o_ref = refs.pop(0)
    if extra:
        ox_ref = refs.pop(0)

    if prologue == "none":
        u = a_ref[...]
    else:
        u_ref = refs.pop(0)

        @pl.when(pl.program_id(1) == 0)
        def _():
            def rows(r, carry):
                sl = pl.ds(pl.multiple_of(r * _PRO_ROWS, _PRO_ROWS), _PRO_ROWS)
                x = a_ref[sl, :]
                if prologue == "mod":
                    shift = mods_ref[row:row + 1, SH1 * d_mod:(SH1 + 1) * d_mod]
                    scale = mods_ref[row:row + 1, SC1 * d_mod:(SC1 + 1) * d_mod]
                    y = _rmsnorm(x, g_ref[...]) * (1.0 + scale) + shift
                else:
                    mu = jnp.mean(x, axis=-1, keepdims=True)
                    xc = x - mu
                    var = jnp.mean(xc * xc, axis=-1, keepdims=True)
                    y = _silu(xc * lax.rsqrt(var + EPS) * g_ref[...] + b_ref[...])
                u_ref[sl, :] = y.astype(BF16)
                return carry

            lax.fori_loop(0, tm // _PRO_ROWS, rows, 0)

        u = u_ref[...]

    if extra:
        @pl.when(pl.program_id(1) == 0)
        def _():
            ox_ref[...] = _dot_nt(u, wx_ref[...].astype(BF16))

    if w_transposed:
        acc = _dot_nt(u, w_ref[...])
    else:
        acc = jnp.dot(u, w_ref[...], preferred_element_type=F32)
    if epilogue == "glu":
        acc2 = jnp.dot(u, w2_ref[...], preferred_element_type=F32)
        acc = acc * _sigmoid(acc2)
    elif epilogue == "residual":
        acc = res_ref[...] + gate_ref[row:row + 1, :] * acc
    o_ref[...] = acc.astype(o_ref.dtype)


def _fused_matmul(a, w, *, n_out, prologue="none", norm=None, ln=None, mods=None, row=0,
                  epilogue="plain", res=None, w2_col_offset=0, extra_w=None, w_t_layer=None,
                  out_dtype=F32, tm=1024, tn=512, name="mm"):
    m, k = a.shape
    tm = min(tm, m)
    tn = min(tn, n_out)
    assert m % tm == 0 and n_out % tn == 0 and w.dtype == BF16 and tm % _PRO_ROWS == 0
    assert w.shape[-1 if w_t_layer is not None else 0] == k
    d_mod = None
    a_mode = {} if prologue == "none" else dict(pipeline_mode=pl.Buffered(1))
    in_specs = [pl.BlockSpec((tm, k), lambda i, j: (i, 0), **a_mode)]
    args = [a]
    if prologue == "mod":
        norm_arr, norm_layer = norm
        mods_all, mods_layer = mods
        d_mod = mods_all.shape[-1] // N_MOD
        in_specs += [pl.BlockSpec((None, 1, k), lambda i, j: (norm_layer, 0, 0)),
                     pl.BlockSpec((None, SUBLANES, mods_all.shape[-1]), lambda i, j: (mods_layer, 0, 0))]
        args += [norm_arr, mods_all]
    elif prologue == "ln_silu":
        (ln_g, ln_b), ln_layer = ln
        in_specs += [pl.BlockSpec((None, 1, k), lambda i, j: (ln_layer, 0, 0))] * 2
        args += [ln_g, ln_b]
    if w_t_layer is not None:
        in_specs.append(pl.BlockSpec((None, tn, k), lambda i, j: (w_t_layer, j, 0)))
    else:
        in_specs.append(pl.BlockSpec((k, tn), lambda i, j: (0, j)))
    args.append(w)
    if epilogue == "glu":
        off = w2_col_offset // tn
        in_specs.append(pl.BlockSpec((k, tn), lambda i, j: (0, j + off)))
        args.append(w)
    elif epilogue == "residual":
        mods_all, mods_layer = mods
        goff = G1 * (mods_all.shape[-1] // N_MOD) // tn
        in_specs += [pl.BlockSpec((tm, tn), lambda i, j: (i, j)),
                     pl.BlockSpec((None, SUBLANES, tn), lambda i, j: (mods_layer, 0, goff + j))]
        args += [res, mods_all]
    out_shape = [jax.ShapeDtypeStruct((m, n_out), out_dtype)]
    out_specs = [pl.BlockSpec((tm, tn), lambda i, j: (i, j))]
    if extra_w is not None:
        in_specs.append(pl.BlockSpec((LANES, k), lambda i, j: (0, 0)))
        args.append(extra_w)
        out_shape.append(jax.ShapeDtypeStruct((m, LANES), F32))
        out_specs.append(pl.BlockSpec((tm, LANES), lambda i, j: (i, 0)))
    scratch = [] if prologue == "none" else [pltpu.VMEM((tm, k), BF16)]
    outs = pl.pallas_call(
        functools.partial(_mm_body, prologue=prologue, epilogue=epilogue, extra=extra_w is not None,
                          w_transposed=w_t_layer is not None,
                          tm=tm, row=row, d_mod=d_mod),
        out_shape=out_shape,
        grid=(m // tm, n_out // tn),
        in_specs=in_specs,
        out_specs=out_specs,
        scratch_shapes=scratch,
        compiler_params=_cparams(("parallel", "arbitrary")),
        name=name,
    )(*args)
    return outs if extra_w is not None else outs[0]


def _scan_body(*refs, reverse, finalize, dk, dv):
    refs = list(refs)
    q_ref, k_ref, v_ref = refs.pop(0), refs.pop(0), refs.pop(0)
    gcol_ref, grow_ref, bias_row_ref, bias_col_ref = refs.pop(0), refs.pop(0), refs.pop(0), refs.pop(0)
    c0_ref, n0_ref, m0_ref = refs.pop(0), refs.pop(0), refs.pop(0)
    if finalize:
        o_ref, hprev_ref, hnorm_ref = refs.pop(0), refs.pop(0), refs.pop(0)
    out_ref, c_ref, n_ref, m_ref = refs.pop(0), refs.pop(0), refs.pop(0), refs.pop(0)

    L = M_CHUNK
    H = M_HEADS

    @pl.when(pl.program_id(0) == 0)
    def _():
        c_ref[...] = c0_ref[...]
        n_ref[...] = n0_ref[...]
        m_ref[...] = m0_ref[...]

    kind_i = 2 if reverse else 0
    kind_f = kind_i + 1

    t_idx = lax.broadcasted_iota(I32, (L, L), 0)
    s_idx = lax.broadcasted_iota(I32, (L, L), 1)
    mask = (s_idx >= t_idx) if reverse else (s_idx <= t_idx)
    mask_f = mask.astype(F32)
    mask_t = ((t_idx >= s_idx) if reverse else (t_idx <= s_idx)).astype(F32)
    q_scale = dk ** -0.5

    gcol = gcol_ref[...] + bias_row_ref[...]
    grow = grow_ref[...] + bias_col_ref[...]
    lf_col = _log_sigmoid(gcol)
    lf_row = _log_sigmoid(grow[kind_f * H:(kind_f + 1) * H, :])
    b_col_all = jnp.dot(mask_f, lf_col, precision=HIGHEST, preferred_element_type=F32)
    b_row_all = jnp.dot(lf_row, mask_t, precision=HIGHEST, preferred_element_type=F32)

    for h in range(H):
        ci, cf = kind_i * H + h, kind_f * H + h
        b_col = b_col_all[:, cf:cf + 1]
        b_row = b_row_all[h:h + 1, :]
        i_col = gcol[:, ci:ci + 1]
        i_row = grow[ci:ci + 1, :]
        m_prev = m_ref[h][:, :1]
        q = q_ref[:, h * dk:(h + 1) * dk]
        k = k_ref[:, h * dk:(h + 1) * dk]
        v = v_ref[:, h * dv:(h + 1) * dv]
        c_state = c_ref[h]
        n_state = n_ref[h]

        dmat = jnp.where(mask, b_col - b_row + i_row, -jnp.inf)
        m_inter = b_col + m_prev
        m_t = jnp.maximum(m_inter, jnp.max(dmat, axis=-1, keepdims=True))
        w_intra = jnp.exp(dmat - m_t) * q_scale
        w_inter = jnp.exp(m_inter - m_t) * q_scale
        s = lax.dot_general(q, k, (((1,), (1,)), ((), ())), preferred_element_type=F32) * w_intra
        num = jnp.dot(s.astype(BF16), v, preferred_element_type=F32)
        num = num + w_inter * jnp.dot(q, c_state.astype(BF16), preferred_element_type=F32)
        qn = jnp.sum(q.astype(F32) * n_state, axis=-1, keepdims=True)
        den = jnp.sum(s, axis=-1, keepdims=True) + w_inter * qn
        hv = num * (1.0 / jnp.maximum(jnp.abs(den), jnp.exp(-m_t)))

        b_last = jnp.sum(lf_row[h:h + 1, :], axis=-1, keepdims=True)
        m_new = jnp.maximum(b_last + m_prev, jnp.max(b_last - b_row + i_row, axis=-1, keepdims=True))
        kw = k.astype(F32) * jnp.exp(b_last - b_col + i_col - m_new)
        decay = jnp.exp(b_last + m_prev - m_new)
        c_ref[h] = decay * c_state + lax.dot_general(
            kw.astype(BF16), v, (((0,), (0,)), ((), ())), preferred_element_type=F32)
        n_ref[h] = decay * n_state + jnp.sum(kw, axis=0, keepdims=True)
        m_ref[h] = jnp.broadcast_to(m_new, (1, LANES))

        cols = slice(h * dv, (h + 1) * dv)
        if finalize:
            ht = hv + hprev_ref[:, cols]
            ht = ht * lax.rsqrt(jnp.mean(ht * ht, axis=-1, keepdims=True) + EPS)
            ht = ht * hnorm_ref[:, cols] * _sigmoid(o_ref[:, cols].astype(F32))
            out_ref[:, cols] = ht.astype(out_ref.dtype)
        else:
            out_ref[:, cols] = hv


def _mlstm_scan(proj, gcol, grow, bias_row, bias_col, state, *, reverse, finalize=None):
    t = proj.shape[0]
    H, L = M_HEADS, M_CHUNK
    c0, n0, m0 = state
    dk, dv = c0.shape[1], c0.shape[2]
    nc = t // L
    cidx = (lambda i: nc - 1 - i) if reverse else (lambda i: i)
    qk_w, v_w = H * dk, H * dv
    assert 2 * qk_w == v_w
    full = lambda shape: pl.BlockSpec(shape, lambda i: (0,) * len(shape))
    in_specs = [
        pl.BlockSpec((L, qk_w), lambda i: (cidx(i), 0)),
        pl.BlockSpec((L, qk_w), lambda i: (cidx(i), 1)),
        pl.BlockSpec((L, v_w), lambda i: (cidx(i), 1)),
        pl.BlockSpec((L, LANES), lambda i: (cidx(i), 0)),
        pl.BlockSpec((LANES, L), lambda i: (0, cidx(i))),
        full((1, LANES)), full((LANES, 1)),
        full(c0.shape), full(n0.shape), full(m0.shape),
    ]
    args = [proj, proj, proj, gcol, grow, bias_row, bias_col, c0, n0, m0]
    if finalize is not None:
        hprev, (hnorm, hn_layer) = finalize
        in_specs += [pl.BlockSpec((L, v_w), lambda i: (cidx(i), 2)),
                     pl.BlockSpec((L, v_w), lambda i: (cidx(i), 0)),
                     pl.BlockSpec((None, 1, v_w), lambda i: (hn_layer, 0, 0))]
        args += [proj, hprev, hnorm]
    out_dtype = BF16 if finalize is not None else F32
    out, c1, n1, m1 = pl.pallas_call(
        functools.partial(_scan_body, reverse=reverse, finalize=finalize is not None, dk=dk, dv=dv),
        out_shape=(jax.ShapeDtypeStruct((t, v_w), out_dtype),
                   jax.ShapeDtypeStruct(c0.shape, F32),
                   jax.ShapeDtypeStruct(n0.shape, F32),
                   jax.ShapeDtypeStruct(m0.shape, F32)),
        grid=(nc,),
        in_specs=in_specs,
        out_specs=(pl.BlockSpec((L, v_w), lambda i: (cidx(i), 0)),
                   full(c0.shape), full(n0.shape), full(m0.shape)),
        compiler_params=_cparams(("arbitrary",)),
        name="mlstm_scan_bwd" if reverse else "mlstm_scan_fwd",
    )(*args)
    return out, (c1, n1, m1)


_CONV_LANES = 512
_CONV_ROWS = 64
_CONV_PAD = 16


def _conv_body(x_ref, w_ref, o_ref, pad_ref, shift_ref, *, seq_len, n_seq):
    zeros = jnp.zeros((_CONV_PAD, _CONV_LANES), F32)
    pad_ref[0:_CONV_PAD, :] = zeros
    pad_ref[_CONV_PAD + seq_len:2 * _CONV_PAD + seq_len, :] = zeros
    n_shift_rows = shift_ref.shape[1]

    def one_seq(s, carry):
        base = pl.multiple_of(s * seq_len, seq_len)
        pad_ref[_CONV_PAD:_CONV_PAD + seq_len, :] = x_ref[pl.ds(base, seq_len), :]
        for b in range(1, SUBLANES):
            shift_ref[b] = pad_ref[b:b + n_shift_rows, :]
        for r0 in range(0, seq_len, _CONV_ROWS):
            acc = jnp.zeros((_CONV_ROWS, _CONV_LANES), F32)
            for j in range(CONV_W):
                start = _CONV_PAD - CONV_HALF + j + r0
                b, a = start % SUBLANES, start - start % SUBLANES
                src = pad_ref if b == 0 else shift_ref.at[b]
                acc = acc + w_ref[j:j + 1, :] * src[a:a + _CONV_ROWS, :]
            o_ref[pl.ds(base + r0, _CONV_ROWS), :] = acc
        return carry

    lax.fori_loop(0, n_seq, one_seq, 0)


def _dwconv(x, w_dw, layer, *, seq_len):
    t, d = x.shape
    n_seq = max(1, min(512, t) // seq_len)
    rows = n_seq * seq_len
    assert t % rows == 0 and d % _CONV_LANES == 0 and seq_len % _CONV_ROWS == 0
    return pl.pallas_call(
        functools.partial(_conv_body, seq_len=seq_len, n_seq=n_seq),
        out_shape=jax.ShapeDtypeStruct((t, d), F32),
        grid=(t // rows, d // _CONV_LANES),
        in_specs=[pl.BlockSpec((rows, _CONV_LANES), lambda i, j: (i, j)),
                  pl.BlockSpec((None, CONV_W, _CONV_LANES), lambda i, j: (layer, 0, j))],
        out_specs=pl.BlockSpec((rows, _CONV_LANES), lambda i, j: (i, j)),
        scratch_shapes=[pltpu.VMEM((seq_len + 2 * _CONV_PAD, _CONV_LANES), F32),
                        pltpu.VMEM((SUBLANES, seq_len + 2 * _CONV_PAD - SUBLANES, _CONV_LANES), F32)],
        compiler_params=_cparams(("parallel", "parallel")),
        name="dwconv",
    )(x, w_dw)


META_IDX, META_RANK, META_W = 0, 8, 16


def _router_body(x_ref, g_ref, mods_ref, wr_ref, rb_ref, swg_ref, swu_ref, swd_ref,
                 u_ref, xb_ref, meta_ref, counts_ref, carry_ref, *, n_experts, row, d, tm):
    @pl.when(pl.program_id(0) == 0)
    def _():
        carry_ref[...] = jnp.zeros_like(carry_ref)

    x = x_ref[...]
    shift = mods_ref[row:row + 1, SH2 * d:(SH2 + 1) * d]
    scale = mods_ref[row:row + 1, SC2 * d:(SC2 + 1) * d]
    gate2 = mods_ref[row:row + 1, G2 * d:(G2 + 1) * d]
    uf = _rmsnorm(x, g_ref[...]) * (1.0 + scale) + shift
    u = uf.astype(BF16)
    u_ref[...] = _pack_bf16_pairs(u)

    hs = _silu(_dot_nt(u, swg_ref[...].astype(BF16)))
    hs = hs * _dot_nt(u, swu_ref[...].astype(BF16))
    shared = jnp.dot(hs.astype(BF16), swd_ref[...].astype(BF16), preferred_element_type=F32)
    xb_ref[...] = x + gate2 * shared

    logits = _dot_nt(u, wr_ref[...].astype(BF16))
    scores = _sigmoid(logits)
    lane = lax.broadcasted_iota(I32, scores.shape, 1)
    sel = jnp.where(lane < n_experts, scores + rb_ref[...], -jnp.inf)
    picked = jnp.zeros_like(scores)
    chosen = jnp.zeros_like(scores)
    firsts = []
    for _ in range(TOP_K):
        mx = jnp.max(sel, axis=-1, keepdims=True)
        first = jnp.min(jnp.where(sel == mx, lane, LANES), axis=-1, keepdims=True)
        hit = lane == first
        picked = jnp.where(hit, scores, picked)
        chosen = jnp.where(hit, 1.0, chosen)
        sel = jnp.where(hit, -jnp.inf, sel)
        firsts.append(first)
    gates = picked / jnp.sum(picked, axis=-1, keepdims=True) * ROUTED_SCALE

    r_idx = lax.broadcasted_iota(I32, (tm, tm), 0)
    c_idx = lax.broadcasted_iota(I32, (tm, tm), 1)
    before = (c_idx < r_idx).astype(BF16)
    pos = jnp.dot(before, chosen.astype(BF16), preferred_element_type=F32) + carry_ref[...]
    meta = jnp.zeros_like(scores)
    for kk, first in enumerate(firsts):
        hit = lane == first
        rank = jnp.sum(jnp.where(hit, pos, 0.0), axis=-1, keepdims=True)
        wk = jnp.sum(jnp.where(hit, gates, 0.0), axis=-1, keepdims=True)
        meta = jnp.where(lane == META_IDX + kk, first.astype(F32), meta)
        meta = jnp.where(lane == META_RANK + kk, rank, meta)
        meta = jnp.where(lane == META_W + kk, wk, meta)
    meta_ref[...] = meta
    carry_ref[...] += jnp.sum(chosen, axis=0, keepdims=True)
    counts_ref[...] = carry_ref[...]


def _router(x, norm_ffn3, layer, mods_all, row, wr_pad, rb_pad, s_w_gate_t, s_w_up_t, s_w_down, n_experts, tm=256):
    t, d = x.shape
    tm = min(tm, t)
    ds_ = s_w_gate_t.shape[1]
    c2 = lambda i: (0, 0)
    lay3 = lambda i: (layer, 0, 0)
    blk = pl.BlockSpec((tm, d), lambda i: (i, 0))
    return pl.pallas_call(
        functools.partial(_router_body, n_experts=n_experts, row=row, d=d, tm=tm),
        out_shape=(jax.ShapeDtypeStruct((t, d // 2), U32), jax.ShapeDtypeStruct((t, d), F32),
                   jax.ShapeDtypeStruct((t, LANES), F32), jax.ShapeDtypeStruct((1, LANES), F32)),
        grid=(t // tm,),
        in_specs=[blk,
                  pl.BlockSpec((None, 1, d), lay3),
                  pl.BlockSpec((None, SUBLANES, mods_all.shape[-1]), lay3),
                  pl.BlockSpec((LANES, d), c2), pl.BlockSpec((1, LANES), c2),
                  pl.BlockSpec((None, ds_, d), lay3), pl.BlockSpec((None, ds_, d), lay3),
                  pl.BlockSpec((None, ds_, d), lay3)],
        out_specs=(pl.BlockSpec((tm, d // 2), lambda i: (i, 0)), blk,
                   pl.BlockSpec((tm, LANES), lambda i: (i, 0)), pl.BlockSpec((1, LANES), c2)),
        scratch_shapes=[pltpu.VMEM((1, LANES), F32)],
        compiler_params=_cparams(("arbitrary",)),
        name="moe_router",
    )(x, norm_ffn3, mods_all, wr_pad, rb_pad, s_w_gate_t, s_w_up_t, s_w_down)


def _dispatch_body(starts_ref, idx_ref, rank_ref, u_ref, xs_hbm, slot_ref, sem, *, tokens):
    def issue(g, carry):
        base = pl.multiple_of(g * SUBLANES, SUBLANES)
        for j in range(SUBLANES):
            for kk in range(TOP_K):
                a = (base + j) * TOP_K + kk
                s = starts_ref[idx_ref[0, a]] + rank_ref[0, a]
                slot_ref[0, a] = s
                pltpu.make_async_copy(u_ref.at[pl.ds(base + j, 1), :], xs_hbm.at[pl.ds(s, 1), :], sem).start()
        return carry

    lax.fori_loop(0, tokens // SUBLANES, issue, 0)
    for _ in range(TOP_K):
        pltpu.make_async_copy(u_ref, xs_hbm.at[pl.ds(0, tokens), :], sem).wait()


def _dispatch(u, starts_ext, idx3, rank3, tokens):
    t, d = u.shape
    smem_blk = pl.BlockSpec((None, 1, tokens * TOP_K), lambda i, st: (i, 0, 0), memory_space=pltpu.SMEM)
    grid_spec = pltpu.PrefetchScalarGridSpec(
        num_scalar_prefetch=1,
        grid=(t // tokens,),
        in_specs=[smem_blk, smem_blk, pl.BlockSpec((tokens, d), lambda i, st: (i, 0))],
        out_specs=(pl.BlockSpec(memory_space=pl.ANY), smem_blk),
        scratch_shapes=[pltpu.SemaphoreType.DMA],
    )
    return pl.pallas_call(
        functools.partial(_dispatch_body, tokens=tokens),
        out_shape=(jax.ShapeDtypeStruct((t * TOP_K, d), u.dtype), jax.ShapeDtypeStruct(idx3.shape, I32)),
        grid_spec=grid_spec,
        compiler_params=_cparams(("arbitrary",)),
        name="moe_dispatch",
    )(starts_ext, idx3, rank3, u)


_FFN_ROWS = 256


def _ffn_body(vt_ref, ve_ref, starts_ref, nvis_ref, xs_ref, wg_ref, wu_ref, wd_ref, ys_ref):
    v = pl.program_id(0)

    @pl.when(v < nvis_ref[0])
    def _():
        tile = vt_ref[v]
        e = ve_ref[v]
        x = _unpack_bf16_pairs(xs_ref[...])
        hg = _dot_nt(x, wg_ref[...].astype(BF16))
        hu = _dot_nt(x, wu_ref[...].astype(BF16))
        hh = (_silu(hg) * hu).astype(BF16)
        y = jnp.dot(hh, wd_ref[...].astype(BF16), preferred_element_type=F32)
        y = _pack_bf16_pairs(y.astype(BF16))
        first = jnp.logical_or(v == 0, vt_ref[jnp.maximum(v - 1, 0)] != tile)

        @pl.when(first)
        def _():
            ys_ref[...] = y

        @pl.when(jnp.logical_not(first))
        def _():
            r = lax.broadcasted_iota(I32, (_FFN_ROWS, 1), 0) + tile * _FFN_ROWS
            mine = jnp.logical_and(r >= starts_ref[e], r < starts_ref[e + 1])
            ys_ref[...] = jnp.where(mine, y, ys_ref[...])


def _grouped_ffn(xs, vt, ve, starts, nvis, e_w_gate_t, e_w_up_t, e_w_down, layer):
    ns = xs.shape[0]
    di, d = e_w_down.shape[2:]
    nv = vt.shape[0]
    w_blk = pl.BlockSpec((None, None, di, d), lambda v, vt, ve, st, nn: (layer, ve[v], 0, 0))
    grid_spec = pltpu.PrefetchScalarGridSpec(
        num_scalar_prefetch=4,
        grid=(nv,),
        in_specs=[pl.BlockSpec((_FFN_ROWS, d // 2), lambda v, vt, ve, st, nn: (vt[v], 0)), w_blk, w_blk, w_blk],
        out_specs=pl.BlockSpec((_FFN_ROWS, d // 2), lambda v, vt, ve, st, nn: (vt[v], 0)),
    )
    return pl.pallas_call(
        _ffn_body,
        out_shape=jax.ShapeDtypeStruct((ns, d // 2), U32),
        grid_spec=grid_spec,
        compiler_params=_cparams(("arbitrary",)),
        name="moe_grouped_ffn",
    )(vt, ve, starts, nvis, xs, e_w_gate_t, e_w_up_t, e_w_down)


_COMB_TOKENS = 128


_COMB_LANES = 1024


def _combine_body(*refs, row, d, final):
    refs = list(refs)
    slot_ref, next_slot_ref = refs.pop(0), refs.pop(0)
    meta_ref, xb_ref, mods_ref = refs.pop(0), refs.pop(0), refs.pop(0)
    fn_ref = refs.pop(0) if final else None
    ys_hbm, o_ref, gbuf_a, gbuf_b, sem_a, sem_b = refs
    half = d // 2
    lanes = min(_COMB_LANES, half)
    tc = _COMB_TOKENS
    i = pl.program_id(0)

    def issue_group(slots, slot_off, g, gbuf, sem):
        base = pl.multiple_of(g * SUBLANES, SUBLANES)
        for j in range(SUBLANES):
            for kk in range(TOP_K):
                s = slots[0, slot_off + (base + j) * TOP_K + kk]
                pltpu.make_async_copy(ys_hbm.at[pl.ds(s, 1), :], gbuf.at[kk, pl.ds(base + j, 1), :], sem).start()

    def wait_all(gbuf, sem):
        for kk in range(TOP_K):
            pltpu.make_async_copy(ys_hbm.at[pl.ds(0, tc), :], gbuf.at[kk], sem).wait()

    def rows(g, gbuf, row_off):
        r = pl.ds(pl.multiple_of(g * SUBLANES, SUBLANES), SUBLANES)
        ro = pl.ds(pl.multiple_of(row_off + g * SUBLANES, SUBLANES), SUBLANES)
        meta = meta_ref[ro, :]
        ssq = jnp.zeros((SUBLANES, 1), F32)
        for c0 in range(0, half, lanes):
            acc_lo = jnp.zeros((SUBLANES, lanes), F32)
            acc_hi = jnp.zeros((SUBLANES, lanes), F32)
            for kk in range(TOP_K):
                p = gbuf[kk, r, c0:c0 + lanes]
                w = meta[:, META_W + kk:META_W + kk + 1]
                acc_lo = acc_lo + w * lax.bitcast_convert_type(p << 16, F32)
                acc_hi = acc_hi + w * lax.bitcast_convert_type(p & jnp.uint32(_HI16), F32)
            for acc, col in ((acc_lo, c0), (acc_hi, half + c0)):
                cols = slice(col, col + lanes)
                gate = mods_ref[row:row + 1, G2 * d + col:G2 * d + col + lanes]
                out = xb_ref[ro, cols] + gate * acc
                o_ref[ro, cols] = out
                if final:
                    ssq = ssq + jnp.sum(out * out, axis=-1, keepdims=True)
        if final:
            o_ref[ro, :] = o_ref[ro, :] * lax.rsqrt(ssq * (1.0 / d) + EPS) * fn_ref[...]

    n_groups = tc // SUBLANES

    @pl.when(i == 0)
    def _():
        def first(g, carry):
            issue_group(slot_ref, 0, g, gbuf_a, sem_a)
            return carry
        lax.fori_loop(0, n_groups, first, 0)

    wait_all(gbuf_a, sem_a)

    def step_a(g, carry):
        issue_group(slot_ref, tc * TOP_K, g, gbuf_b, sem_b)
        rows(g, gbuf_a, 0)
        return carry
    lax.fori_loop(0, n_groups, step_a, 0)

    wait_all(gbuf_b, sem_b)

    def step_b(g, carry):
        issue_group(next_slot_ref, 0, g, gbuf_a, sem_a)
        rows(g, gbuf_b, tc)
        return carry
    lax.fori_loop(0, n_groups, step_b, 0)

    @pl.when(i == pl.num_programs(0) - 1)
    def _():
        wait_all(gbuf_a, sem_a)


def _combine(ys, slot3, meta, xb, mods_all, layer, row, final_norm=None):
    t, d = xb.shape
    tc = _COMB_TOKENS
    nb = t // (2 * tc)
    assert t % (2 * tc) == 0 and (d // 2) % min(_COMB_LANES, d // 2) == 0
    blk = pl.BlockSpec((2 * tc, d), lambda i: (i, 0))
    in_specs = [pl.BlockSpec((None, 1, 2 * tc * TOP_K), lambda i: (i, 0, 0), memory_space=pltpu.SMEM),
                pl.BlockSpec((None, 1, 2 * tc * TOP_K), lambda i: (jnp.minimum(i + 1, nb - 1), 0, 0),
                             memory_space=pltpu.SMEM),
                pl.BlockSpec((2 * tc, LANES), lambda i: (i, 0)), blk,
                pl.BlockSpec((None, SUBLANES, mods_all.shape[-1]), lambda i: (layer, 0, 0))]
    args = [slot3, slot3, meta, xb, mods_all]
    if final_norm is not None:
        in_specs.append(pl.BlockSpec((1, d), lambda i: (0, 0)))
        args.append(final_norm)
    in_specs.append(pl.BlockSpec(memory_space=pl.ANY))
    args.append(ys)
    return pl.pallas_call(
        functools.partial(_combine_body, row=row, d=d, final=final_norm is not None),
        out_shape=jax.ShapeDtypeStruct((t, d), F32),
        grid=(nb,),
        in_specs=in_specs,
        out_specs=blk,
        scratch_shapes=[pltpu.VMEM((TOP_K, tc, d // 2), U32), pltpu.VMEM((TOP_K, tc, d // 2), U32),
                        pltpu.SemaphoreType.DMA, pltpu.SemaphoreType.DMA],
        compiler_params=_cparams(("arbitrary",)),
        name="moe_combine",
    )(*args)


def _moe_layer(x, layer, row, mods_all, norm_ffn3, wr_pad, rb_pad, e_w_gate_t, e_w_up_t, e_w_down,
               s_w_gate_t, s_w_up_t, s_w_down, final_norm):
    t, d = x.shape
    n_experts = e_w_down.shape[1]
    u, xb, meta, counts = _router(x, norm_ffn3, layer, mods_all, row, wr_pad, rb_pad,
                                  s_w_gate_t, s_w_up_t, s_w_down, n_experts)
    counts = counts[0, :n_experts].astype(I32)
    ends = jnp.cumsum(counts)
    starts = ends - counts
    idx = meta[:, META_IDX:META_IDX + TOP_K].astype(I32).reshape(-1)
    rank = meta[:, META_RANK:META_RANK + TOP_K].astype(I32).reshape(-1)
    ns = t * TOP_K
    n_tiles = ns // _FFN_ROWS
    first_tile = starts // _FFN_ROWS
    ntile_e = jnp.where(counts > 0, (ends - 1) // _FFN_ROWS - first_tile + 1, 0)
    vend = jnp.cumsum(ntile_e)
    vstart = vend - ntile_e
    nvis = vend[-1]
    nv = n_tiles + n_experts - 1
    v = jnp.minimum(jnp.arange(nv, dtype=I32), nvis - 1)
    ve = jnp.minimum(jnp.sum((vend[None, :] <= v[:, None]).astype(I32), axis=1), n_experts - 1)
    onehot = (ve[:, None] == jnp.arange(n_experts, dtype=I32)[None, :]).astype(I32)
    vt = (jnp.sum(onehot * (first_tile - vstart)[None, :], axis=1) + v).astype(I32)
    starts_ext = jnp.concatenate([starts, ends[-1:]]).astype(I32)

    disp_tokens = min(256, t)
    per = lambda a, n: a.reshape(t // n, 1, n * TOP_K)
    xs, slot3 = _dispatch(u, starts_ext, per(idx, disp_tokens), per(rank, disp_tokens), disp_tokens)
    ys = _grouped_ffn(xs, vt, ve, starts_ext, nvis.reshape(1).astype(I32), e_w_gate_t, e_w_up_t, e_w_down, layer)
    return _combine(ys, per(slot3, 2 * _COMB_TOKENS), meta, xb, mods_all, layer, row, final_norm)


def _pad_cols(a, n):
    return jnp.pad(a, ((0, 0), (0, n - a.shape[1])))


def _mlstm_layer(streams, mods, norm, a_w_in, j, gate_bias, head_norm3, a_w_out, ctx_out):
    H = M_HEADS
    d = a_w_in.shape[1]
    dv = head_norm3.shape[-1] // H
    dk = dv // 2
    n_proj = 2 * H * dk + 2 * H * dv
    a_w_in_t = jnp.swapaxes(a_w_in, 1, 2)
    w_gates = jnp.pad(a_w_in_t[j, n_proj:], ((0, LANES - 4 * H), (0, 0)))
    w_in_t = a_w_in_t.astype(BF16)
    w_out = a_w_out[j].astype(BF16)
    bias_row = _pad_cols(gate_bias.reshape(1, 4 * H), LANES)
    bias_col = bias_row.reshape(LANES, 1)

    projs = []
    for xs, row in streams:
        proj, gates = _fused_matmul(xs, w_in_t, w_t_layer=j, n_out=n_proj, prologue="mod", norm=norm, mods=mods, row=row,
                                    extra_w=w_gates, out_dtype=BF16, name="mlstm_in_proj")
        projs.append((proj, gates, gates.T))

    zero = (jnp.zeros((H, dk, dv), F32), jnp.zeros((H, 1, dk), F32), jnp.zeros((H, 1, LANES), F32))
    (px, gx, gxt), (pc, gc, gct) = projs
    hn = (head_norm3, j)
    hcf, st_f = _mlstm_scan(pc, gc, gct, bias_row, bias_col, zero, reverse=False)
    hxf, _ = _mlstm_scan(px, gx, gxt, bias_row, bias_col, st_f, reverse=False)
    yc_pre, st_b = _mlstm_scan(pc, gc, gct, bias_row, bias_col, zero, reverse=True, finalize=(hcf, hn))
    yx_pre, _ = _mlstm_scan(px, gx, gxt, bias_row, bias_col, st_b, reverse=True, finalize=(hxf, hn))

    outs = []
    for (xs, row), y_pre, live in zip(streams, (yx_pre, yc_pre), (True, ctx_out)):
        if live:
            outs.append(_fused_matmul(y_pre, w_out, n_out=d, epilogue="residual", res=xs, mods=mods,
                                      row=row, name="mlstm_out_proj"))
        else:
            outs.append(xs)
    return outs


def _conv_layer(streams, mods, norm, b_w_in, b_w_dw, ln, b_w_out, j, live_flags, seq_lens):
    d = b_w_in.shape[1]
    w_in = b_w_in[j].astype(BF16)
    w_out = b_w_out[j].astype(BF16)
    outs = []
    for (xs, row), live, seq_len in zip(streams, live_flags, seq_lens):
        if not live:
            outs.append(xs)
            continue
        hglu = _fused_matmul(xs, w_in, n_out=d, prologue="mod", norm=norm, mods=mods, row=row,
                             epilogue="glu", w2_col_offset=d, name="conv_in_glu")
        hc = _dwconv(hglu, b_w_dw, j, seq_len=seq_len)
        outs.append(_fused_matmul(hc, w_out, n_out=d, prologue="ln_silu", ln=(ln, j),
                                  epilogue="residual", res=xs, mods=mods, row=row, name="conv_out_proj"))
    return outs


def kernel(x, c, ctx, c_ctx, ada_down, ada_up, ada_bias, norm_mix, norm_ffn, a_w_in, a_gate_bias, a_head_norm, a_w_out, b_w_in, b_w_dw, b_ln_g, b_ln_b, b_w_out, w_router, router_bias, e_w_gate, e_w_up, e_w_down, s_w_gate, s_w_up, s_w_down, final_norm):
    bsz, t, d = x.shape
    assert bsz == 1 and c.shape[0] == 1 and ctx.shape[0] == 1
    depth = ada_down.shape[0]
    n_experts = w_router.shape[-1]
    t_ctx = ctx.shape[1]

    cond = jnp.zeros((SUBLANES, d), F32).at[0].set(c[0]).at[1].set(c_ctx)
    mods_all = _ada_all(cond, ada_down, ada_up, ada_bias)
    norm_mix3 = norm_mix.reshape(depth, 1, d)
    norm_ffn3 = norm_ffn.reshape(depth, 1, d)
    head_norm3 = a_head_norm.reshape(a_head_norm.shape[0], 1, -1)
    ln3 = (b_ln_g.reshape(-1, 1, d), b_ln_b.reshape(-1, 1, d))
    fn = final_norm.reshape(1, d)
    e_gate_t, e_up_t = jnp.swapaxes(e_w_gate, 2, 3), jnp.swapaxes(e_w_up, 2, 3)
    s_gate_t, s_up_t = jnp.swapaxes(s_w_gate, 1, 2), jnp.swapaxes(s_w_up, 1, 2)

    xs, cs = x[0], ctx[0]
    for i in range(depth):
        kind, j = i % 2, i // 2
        ctx_live = any(l % 2 == 0 for l in range(i + 1, depth))
        mods = (mods_all, i)
        norm = (norm_mix3, i)
        streams = [(xs, 0), (cs, 1)]
        if kind == 0:
            xs, cs = _mlstm_layer(streams, mods, norm, a_w_in, j, a_gate_bias[j], head_norm3, a_w_out, ctx_live)
        else:
            xs, cs = _conv_layer(streams, mods, norm, b_w_in, b_w_dw, ln3, b_w_out, j,
                                 (True, ctx_live), (GRID_W, t_ctx))

        wr = jnp.pad(w_router[i].T, ((0, LANES - n_experts), (0, 0)))
        rb = _pad_cols(router_bias[i].reshape(1, n_experts), LANES)
        last = i == depth - 1
        new = []
        for (s, row), live in zip(((xs, 0), (cs, 1)), (True, ctx_live)):
            if live:
                s = _moe_layer(s, i, row, mods_all, norm_ffn3, wr, rb, e_gate_t, e_up_t, e_w_down,
                               s_gate_t, s_up_t, s_w_down, fn if (last and row == 0) else None)
            new.append(s)
        xs, cs = new
    return xs[None]
```

```python
import functools

import jax
import jax.numpy as jnp
from jax import lax
from jax.experimental import pallas as pl
from jax.experimental.pallas import tpu as pltpu

F32 = jnp.float32
BF16 = jnp.bfloat16
I32 = jnp.int32
U32 = jnp.uint32

GRID_W = 64
M_HEADS = 8
M_CHUNK = 128
CONV_W = 31
CONV_HALF = CONV_W // 2
TOP_K = 6
ROUTED_SCALE = 2.5
N_MOD = 6
EPS = 1e-6
SH1, SC1, G1, SH2, SC2, G2 = range(N_MOD)

LANES = 128
SUBLANES = 8
VMEM_LIMIT = 56 * 1024 * 1024
HIGHEST = lax.Precision.HIGHEST


def _cparams(sem):
    return pltpu.CompilerParams(dimension_semantics=sem, vmem_limit_bytes=VMEM_LIMIT)


def _sigmoid(v):
    return 1.0 / (1.0 + jnp.exp(-v))


def _silu(v):
    return v * _sigmoid(v)


def _log_sigmoid(v):
    return jnp.minimum(v, 0.0) - jnp.log(1.0 + jnp.exp(-jnp.abs(v)))


def _dot_nt(a, b_t):
    return lax.dot_general(a, b_t, (((1,), (1,)), ((), ())), preferred_element_type=F32)


_HI16 = 0xFFFF0000


def _pack_bf16_pairs(u):
    half = u.shape[1] // 2
    bits = lax.bitcast_convert_type(u.astype(F32), U32)
    return (bits[:, half:] & jnp.uint32(_HI16)) | (bits[:, :half] >> 16)


def _unpack_bf16_pairs(p):
    lo = lax.bitcast_convert_type(p << 16, F32)
    hi = lax.bitcast_convert_type(p & jnp.uint32(_HI16), F32)
    return jnp.concatenate([lo, hi], axis=1).astype(BF16)


def _rmsnorm(x, g):
    return x * lax.rsqrt(jnp.mean(x * x, axis=-1, keepdims=True) + EPS) * g


def _ada_body(cond_ref, down_ref, up_ref, bias_ref, o_ref):
    s = _silu(cond_ref[...])
    t = jnp.dot(s, down_ref[...], precision=HIGHEST, preferred_element_type=F32)
    o_ref[...] = jnp.dot(t, up_ref[...], precision=HIGHEST, preferred_element_type=F32) + bias_ref[...]


def _ada_all(cond, ada_down, ada_up, ada_bias):
    n_layers, d, rank = ada_down.shape
    n_out = ada_up.shape[-1]
    tn = d
    assert n_out % tn == 0
    return pl.pallas_call(
        _ada_body,
        out_shape=jax.ShapeDtypeStruct((n_layers, SUBLANES, n_out), F32),
        grid=(n_layers, n_out // tn),
        in_specs=[
            pl.BlockSpec((SUBLANES, d), lambda l, j: (0, 0)),
            pl.BlockSpec((None, d, rank), lambda l, j: (l, 0, 0)),
            pl.BlockSpec((None, rank, tn), lambda l, j: (l, 0, j)),
            pl.BlockSpec((None, 1, tn), lambda l, j: (l, 0, j)),
        ],
        out_specs=pl.BlockSpec((None, SUBLANES, tn), lambda l, j: (l, 0, j)),
        compiler_params=_cparams(("parallel", "arbitrary")),
        name="ada_mod",
    )(cond, ada_down, ada_up, ada_bias.reshape(n_layers, 1, n_out))


_PRO_ROWS = 32


def _mm_body(*refs, prologue, epilogue, extra, tm, row, d_mod, w_transposed):
    refs = list(refs)
    a_ref = refs.pop(0)
    if prologue == "mod":
        g_ref, mods_ref = refs.pop(0), refs.pop(0)
    elif prologue == "ln_silu":
        g_ref, b_ref = refs.pop(0), refs.pop(0)
    w_ref = refs.pop(0)
    if epilogue == "glu":
        w2_ref = refs.pop(0)
    elif epilogue == "residual":
        res_ref, gate_ref = refs.pop(0), refs.pop(0)
    if extra:
        wx_ref = refs.pop(0)
    o_ref = refs.pop(0)
    if extra:
        ox_ref = refs.pop(0)

    if prologue == "none":
        u = a_ref[...]
    else:
        u_ref = refs.pop(0)

        @pl.when(pl.program_id(1) == 0)
        def _():
            def rows(r, carry):
                sl = pl.ds(pl.multiple_of(r * _PRO_ROWS, _PRO_ROWS), _PRO_ROWS)
                x = a_ref[sl, :]
                if prologue == "mod":
                    shift = mods_ref[row:row + 1, SH1 * d_mod:(SH1 + 1) * d_mod]
                    scale = mods_ref[row:row + 1, SC1 * d_mod:(SC1 + 1) * d_mod]
                    y = _rmsnorm(x, g_ref[...]) * (1.0 + scale) + shift
                else:
                    mu = jnp.mean(x, axis=-1, keepdims=True)
                    xc = x - mu
                    var = jnp.mean(xc * xc, axis=-1, keepdims=True)
                    y = _silu(xc * lax.rsqrt(var + EPS) * g_ref[...] + b_ref[...])
                u_ref[sl, :] = y.astype(BF16)
                return carry

            lax.fori_loop(0, tm // _PRO_ROWS, rows, 0)

        u = u_ref[...]

    if extra:
        @pl.when(pl.program_id(1) == 0)
        def _():
            ox_ref[...] = _dot_nt(u, wx_ref[...].astype(BF16))

    if w_transposed:
        acc = _dot_nt(u, w_ref[...])
    else:
        acc = jnp.dot(u, w_ref[...], preferred_element_type=F32)
    if epilogue == "glu":
        acc2 = jnp.dot(u, w2_ref[...], preferred_element_type=F32)
        acc = acc * _sigmoid(acc2)
    elif epilogue == "residual":
        acc = res_ref[...] + gate_ref[row:row + 1, :] * acc
    o_ref[...] = acc.astype(o_ref.dtype)


def _fused_matmul(a, w, *, n_out, prologue="none", norm=None, ln=None, mods=None, row=0,
                  epilogue="plain", res=None, w2_col_offset=0, extra_w=None, w_t_layer=None,
                  out_dtype=F32, tm=1024, tn=512, name="mm"):
    m, k = a.shape
    tm = min(tm, m)
    tn = min(tn, n_out)
    assert m % tm == 0 and n_out % tn == 0 and w.dtype == BF16 and tm % _PRO_ROWS == 0
    assert w.shape[-1 if w_t_layer is not None else 0] == k
    d_mod = None
    a_mode = {} if prologue == "none" else dict(pipeline_mode=pl.Buffered(1))
    in_specs = [pl.BlockSpec((tm, k), lambda i, j: (i, 0), **a_mode)]
    args = [a]
    if prologue == "mod":
        norm_arr, norm_layer = norm
        mods_all, mods_layer = mods
        d_mod = mods_all.shape[-1] // N_MOD
        in_specs += [pl.BlockSpec((None, 1, k), lambda i, j: (norm_layer, 0, 0)),
                     pl.BlockSpec((None, SUBLANES, mods_all.shape[-1]), lambda i, j: (mods_layer, 0, 0))]
        args += [norm_arr, mods_all]
    elif prologue == "ln_silu":
        (ln_g, ln_b), ln_layer = ln
        in_specs += [pl.BlockSpec((None, 1, k), lambda i, j: (ln_layer, 0, 0))] * 2
        args += [ln_g, ln_b]
    if w_t_layer is not None:
        in_specs.append(pl.BlockSpec((None, tn, k), lambda i, j: (w_t_layer, j, 0)))
    else:
        in_specs.append(pl.BlockSpec((k, tn), lambda i, j: (0, j)))
    args.append(w)
    if epilogue == "glu":
        off = w2_col_offset // tn
        in_specs.append(pl.BlockSpec((k, tn), lambda i, j: (0, j + off)))
        args.append(w)
    elif epilogue == "residual":
        mods_all, mods_layer = mods
        goff = G1 * (mods_all.shape[-1] // N_MOD) // tn
        in_specs += [pl.BlockSpec((tm, tn), lambda i, j: (i, j)),
                     pl.BlockSpec((None, SUBLANES, tn), lambda i, j: (mods_layer, 0, goff + j))]
        args += [res, mods_all]
    out_shape = [jax.ShapeDtypeStruct((m, n_out), out_dtype)]
    out_specs = [pl.BlockSpec((tm, tn), lambda i, j: (i, j))]
    if extra_w is not None:
        in_specs.append(pl.BlockSpec((LANES, k), lambda i, j: (0, 0)))
        args.append(extra_w)
        out_shape.append(jax.ShapeDtypeStruct((m, LANES), F32))
        out_specs.append(pl.BlockSpec((tm, LANES), lambda i, j: (i, 0)))
    scratch = [] if prologue == "none" else [pltpu.VMEM((tm, k), BF16)]
    outs = pl.pallas_call(
        functools.partial(_mm_body, prologue=prologue, epilogue=epilogue, extra=extra_w is not None,
                          w_transposed=w_t_layer is not None,
                          tm=tm, row=row, d_mod=d_mod),
        out_shape=out_shape,
        grid=(m // tm, n_out // tn),
        in_specs=in_specs,
        out_specs=out_specs,
        scratch_shapes=scratch,
        compiler_params=_cparams(("parallel", "arbitrary")),
        name=name,
    )(*args)
    return outs if extra_w is not None else outs[0]


def _scan_body(*refs, reverse, finalize, dk, dv):
    refs = list(refs)
    q_ref, k_ref, v_ref = refs.pop(0), refs.pop(0), refs.pop(0)
    gcol_ref, grow_ref, bias_row_ref, bias_col_ref = refs.pop(0), refs.pop(0), refs.pop(0), refs.pop(0)
    c0_ref, n0_ref, m0_ref = refs.pop(0), refs.pop(0), refs.pop(0)
    if finalize:
        o_ref, hprev_ref, hnorm_ref = refs.pop(0), refs.pop(0), refs.pop(0)
    out_ref, c_ref, n_ref, m_ref = refs.pop(0), refs.pop(0), refs.pop(0), refs.pop(0)

    L = M_CHUNK
    H = M_HEADS

    @pl.when(pl.program_id(0) == 0)
    def _():
        c_ref[...] = c0_ref[...]
        n_ref[...] = n0_ref[...]
        m_ref[...] = m0_ref[...]

    kind_i = 2 if reverse else 0
    kind_f = kind_i + 1

    t_idx = lax.broadcasted_iota(I32, (L, L), 0)
    s_idx = lax.broadcasted_iota(I32, (L, L), 1)
    mask = (s_idx >= t_idx) if reverse else (s_idx <= t_idx)
    mask_f = mask.astype(F32)
    mask_t = ((t_idx >= s_idx) if reverse else (t_idx <= s_idx)).astype(F32)
    q_scale = dk ** -0.5

    gcol = gcol_ref[...] + bias_row_ref[...]
    grow = grow_ref[...] + bias_col_ref[...]
    lf_col = _log_sigmoid(gcol)
    lf_row = _log_sigmoid(grow[kind_f * H:(kind_f + 1) * H, :])
    b_col_all = jnp.dot(mask_f, lf_col, precision=HIGHEST, preferred_element_type=F32)
    b_row_all = jnp.dot(lf_row, mask_t, precision=HIGHEST, preferred_element_type=F32)

    qs = lambda h: q_ref[:, h * dk:(h + 1) * dk]
    ks = lambda h: k_ref[:, h * dk:(h + 1) * dk]
    vs = lambda h: v_ref[:, h * dv:(h + 1) * dv]

    s_raw = [_dot_nt(qs(h), ks(h)) for h in range(H)]
    qc = [jnp.dot(qs(h), c_ref[h].astype(BF16), preferred_element_type=F32) for h in range(H)]

    st = []
    for h in range(H):
        ci, cf = kind_i * H + h, kind_f * H + h
        b_col = b_col_all[:, cf:cf + 1]
        b_row = b_row_all[h:h + 1, :]
        i_col = gcol[:, ci:ci + 1]
        i_row = grow[ci:ci + 1, :]
        m_prev = m_ref[h][:, :1]
        dmat = jnp.where(mask, b_col - b_row + i_row, -jnp.inf)
        m_inter = b_col + m_prev
        m_t = jnp.maximum(m_inter, jnp.max(dmat, axis=-1, keepdims=True))
        w_intra = jnp.exp(dmat - m_t) * q_scale
        w_inter = jnp.exp(m_inter - m_t) * q_scale
        b_last = jnp.sum(lf_row[h:h + 1, :], axis=-1, keepdims=True)
        m_new = jnp.maximum(b_last + m_prev, jnp.max(b_last - b_row + i_row, axis=-1, keepdims=True))
        kw = ks(h).astype(F32) * jnp.exp(b_last - b_col + i_col - m_new)
        decay = jnp.exp(b_last + m_prev - m_new)
        s = s_raw[h] * w_intra
        st.append(dict(m_t=m_t, w_inter=w_inter, m_new=m_new, kw=kw, decay=decay, s=s))

    sv = [jnp.dot(st[h]["s"].astype(BF16), vs(h), preferred_element_type=F32) for h in range(H)]
    kv = [lax.dot_general(st[h]["kw"].astype(BF16), vs(h), (((0,), (0,)), ((), ())),
                          preferred_element_type=F32) for h in range(H)]

    for h in range(H):
        a = st[h]
        n_state = n_ref[h]
        num = sv[h] + a["w_inter"] * qc[h]
        qn = jnp.sum(qs(h).astype(F32) * n_state, axis=-1, keepdims=True)
        den = jnp.sum(a["s"], axis=-1, keepdims=True) + a["w_inter"] * qn
        hv = num * (1.0 / jnp.maximum(jnp.abs(den), jnp.exp(-a["m_t"])))

        c_ref[h] = a["decay"] * c_ref[h] + kv[h]
        n_ref[h] = a["decay"] * n_state + jnp.sum(a["kw"], axis=0, keepdims=True)
        m_ref[h] = jnp.broadcast_to(a["m_new"], (1, LANES))

        cols = slice(h * dv, (h + 1) * dv)
        if finalize:
            ht = hv + hprev_ref[:, cols]
            ht = ht * lax.rsqrt(jnp.mean(ht * ht, axis=-1, keepdims=True) + EPS)
            ht = ht * hnorm_ref[:, cols] * _sigmoid(o_ref[:, cols].astype(F32))
            out_ref[:, cols] = ht.astype(out_ref.dtype)
        else:
            out_ref[:, cols] = hv


def _mlstm_scan(proj, gcol, grow, bias_row, bias_col, state, *, reverse, finalize=None):
    t = proj.shape[0]
    H, L = M_HEADS, M_CHUNK
    c0, n0, m0 = state
    dk, dv = c0.shape[1], c0.shape[2]
    nc = t // L
    cidx = (lambda i: nc - 1 - i) if reverse else (lambda i: i)
    qk_w, v_w = H * dk, H * dv
    assert 2 * qk_w == v_w
    full = lambda shape: pl.BlockSpec(shape, lambda i: (0,) * len(shape))
    in_specs = [
        pl.BlockSpec((L, qk_w), lambda i: (cidx(i), 0)),
        pl.BlockSpec((L, qk_w), lambda i: (cidx(i), 1)),
        pl.BlockSpec((L, v_w), lambda i: (cidx(i), 1)),
        pl.BlockSpec((L, LANES), lambda i: (cidx(i), 0)),
        pl.BlockSpec((LANES, L), lambda i: (0, cidx(i))),
        full((1, LANES)), full((LANES, 1)),
        full(c0.shape), full(n0.shape), full(m0.shape),
    ]
    args = [proj, proj, proj, gcol, grow, bias_row, bias_col, c0, n0, m0]
    if finalize is not None:
        hprev, (hnorm, hn_layer) = finalize
        in_specs += [pl.BlockSpec((L, v_w), lambda i: (cidx(i), 2)),
                     pl.BlockSpec((L, v_w), lambda i: (cidx(i), 0)),
                     pl.BlockSpec((None, 1, v_w), lambda i: (hn_layer, 0, 0))]
        args += [proj, hprev, hnorm]
    out_dtype = BF16 if finalize is not None else F32
    out, c1, n1, m1 = pl.pallas_call(
        functools.partial(_scan_body, reverse=reverse, finalize=finalize is not None, dk=dk, dv=dv),
        out_shape=(jax.ShapeDtypeStruct((t, v_w), out_dtype),
                   jax.ShapeDtypeStruct(c0.shape, F32),
                   jax.ShapeDtypeStruct(n0.shape, F32),
                   jax.ShapeDtypeStruct(m0.shape, F32)),
        grid=(nc,),
        in_specs=in_specs,
        out_specs=(pl.BlockSpec((L, v_w), lambda i: (cidx(i), 0)),
                   full(c0.shape), full(n0.shape), full(m0.shape)),
        compiler_params=_cparams(("arbitrary",)),
        name="mlstm_scan_bwd" if reverse else "mlstm_scan_fwd",
    )(*args)
    return out, (c1, n1, m1)


_CONV_LANES = 512
_CONV_ROWS = 64
_CONV_PAD = 16


def _conv_body(x_ref, w_ref, o_ref, pad_ref, shift_ref, *, seq_len, n_seq):
    zeros = jnp.zeros((_CONV_PAD, _CONV_LANES), F32)
    pad_ref[0:_CONV_PAD, :] = zeros
    pad_ref[_CONV_PAD + seq_len:2 * _CONV_PAD + seq_len, :] = zeros
    n_shift_rows = shift_ref.shape[1]

    def one_seq(s, carry):
        base = pl.multiple_of(s * seq_len, seq_len)
        pad_ref[_CONV_PAD:_CONV_PAD + seq_len, :] = x_ref[pl.ds(base, seq_len), :]
        for b in range(1, SUBLANES):
            shift_ref[b] = pad_ref[b:b + n_shift_rows, :]
        for r0 in range(0, seq_len, _CONV_ROWS):
            acc = jnp.zeros((_CONV_ROWS, _CONV_LANES), F32)
            for j in range(CONV_W):
                start = _CONV_PAD - CONV_HALF + j + r0
                b, a = start % SUBLANES, start - start % SUBLANES
                src = pad_ref if b == 0 else shift_ref.at[b]
                acc = acc + w_ref[j:j + 1, :] * src[a:a + _CONV_ROWS, :]
            o_ref[pl.ds(base + r0, _CONV_ROWS), :] = acc
        return carry

    lax.fori_loop(0, n_seq, one_seq, 0)


def _dwconv(x, w_dw, layer, *, seq_len):
    t, d = x.shape
    n_seq = max(1, min(512, t) // seq_len)
    rows = n_seq * seq_len
    assert t % rows == 0 and d % _CONV_LANES == 0 and seq_len % _CONV_ROWS == 0
    return pl.pallas_call(
        functools.partial(_conv_body, seq_len=seq_len, n_seq=n_seq),
        out_shape=jax.ShapeDtypeStruct((t, d), F32),
        grid=(t // rows, d // _CONV_LANES),
        in_specs=[pl.BlockSpec((rows, _CONV_LANES), lambda i, j: (i, j)),
                  pl.BlockSpec((None, CONV_W, _CONV_LANES), lambda i, j: (layer, 0, j))],
        out_specs=pl.BlockSpec((rows, _CONV_LANES), lambda i, j: (i, j)),
        scratch_shapes=[pltpu.VMEM((seq_len + 2 * _CONV_PAD, _CONV_LANES), F32),
                        pltpu.VMEM((SUBLANES, seq_len + 2 * _CONV_PAD - SUBLANES, _CONV_LANES), F32)],
        compiler_params=_cparams(("parallel", "parallel")),
        name="dwconv",
    )(x, w_dw)


META_IDX, META_RANK, META_W = 0, 8, 16


def _router_body(x_ref, g_ref, mods_ref, wr_ref, rb_ref, swg_ref, swu_ref, swd_ref,
                 u_ref, xb_ref, meta_ref, counts_ref, carry_ref, *, n_experts, row, d, tm):
    @pl.when(pl.program_id(0) == 0)
    def _():
        carry_ref[...] = jnp.zeros_like(carry_ref)

    x = x_ref[...]
    shift = mods_ref[row:row + 1, SH2 * d:(SH2 + 1) * d]
    scale = mods_ref[row:row + 1, SC2 * d:(SC2 + 1) * d]
    gate2 = mods_ref[row:row + 1, G2 * d:(G2 + 1) * d]
    uf = _rmsnorm(x, g_ref[...]) * (1.0 + scale) + shift
    u = uf.astype(BF16)
    u_ref[...] = _pack_bf16_pairs(u)

    hs = _silu(_dot_nt(u, swg_ref[...].astype(BF16)))
    hs = hs * _dot_nt(u, swu_ref[...].astype(BF16))
    shared = jnp.dot(hs.astype(BF16), swd_ref[...].astype(BF16), preferred_element_type=F32)
    xb_ref[...] = x + gate2 * shared

    logits = _dot_nt(u, wr_ref[...].astype(BF16))
    scores = _sigmoid(logits)
    lane = lax.broadcasted_iota(I32, scores.shape, 1)
    sel = jnp.where(lane < n_experts, scores + rb_ref[...], -jnp.inf)
    picked = jnp.zeros_like(scores)
    chosen = jnp.zeros_like(scores)
    firsts = []
    for _ in range(TOP_K):
        mx = jnp.max(sel, axis=-1, keepdims=True)
        first = jnp.min(jnp.where(sel == mx, lane, LANES), axis=-1, keepdims=True)
        hit = lane == first
        picked = jnp.where(hit, scores, picked)
        chosen = jnp.where(hit, 1.0, chosen)
        sel = jnp.where(hit, -jnp.inf, sel)
        firsts.append(first)
    gates = picked / jnp.sum(picked, axis=-1, keepdims=True) * ROUTED_SCALE

    r_idx = lax.broadcasted_iota(I32, (tm, tm), 0)
    c_idx = lax.broadcasted_iota(I32, (tm, tm), 1)
    before = (c_idx < r_idx).astype(BF16)
    pos = jnp.dot(before, chosen.astype(BF16), preferred_element_type=F32) + carry_ref[...]
    meta = jnp.zeros_like(scores)
    for kk, first in enumerate(firsts):
        hit = lane == first
        rank = jnp.sum(jnp.where(hit, pos, 0.0), axis=-1, keepdims=True)
        wk = jnp.sum(jnp.where(hit, gates, 0.0), axis=-1, keepdims=True)
        meta = jnp.where(lane == META_IDX + kk, first.astype(F32), meta)
        meta = jnp.where(lane == META_RANK + kk, rank, meta)
        meta = jnp.where(lane == META_W + kk, wk, meta)
    meta_ref[...] = meta
    carry_ref[...] += jnp.sum(chosen, axis=0, keepdims=True)
    counts_ref[...] = carry_ref[...]


def _router(x, norm_ffn3, layer, mods_all, row, wr_pad, rb_pad, s_w_gate_t, s_w_up_t, s_w_down, n_experts, tm=256):
    t, d = x.shape
    tm = min(tm, t)
    ds_ = s_w_gate_t.shape[1]
    c2 = lambda i: (0, 0)
    lay3 = lambda i: (layer, 0, 0)
    blk = pl.BlockSpec((tm, d), lambda i: (i, 0))
    return pl.pallas_call(
        functools.partial(_router_body, n_experts=n_experts, row=row, d=d, tm=tm),
        out_shape=(jax.ShapeDtypeStruct((t, d // 2), U32), jax.ShapeDtypeStruct((t, d), F32),
                   jax.ShapeDtypeStruct((t, LANES), F32), jax.ShapeDtypeStruct((1, LANES), F32)),
        grid=(t // tm,),
        in_specs=[blk,
                  pl.BlockSpec((None, 1, d), lay3),
                  pl.BlockSpec((None, SUBLANES, mods_all.shape[-1]), lay3),
                  pl.BlockSpec((LANES, d), c2), pl.BlockSpec((1, LANES), c2),
                  pl.BlockSpec((None, ds_, d), lay3), pl.BlockSpec((None, ds_, d), lay3),
                  pl.BlockSpec((None, ds_, d), lay3)],
        out_specs=(pl.BlockSpec((tm, d // 2), lambda i: (i, 0)), blk,
                   pl.BlockSpec((tm, LANES), lambda i: (i, 0)), pl.BlockSpec((1, LANES), c2)),
        scratch_shapes=[pltpu.VMEM((1, LANES), F32)],
        compiler_params=_cparams(("arbitrary",)),
        name="moe_router",
    )(x, norm_ffn3, mods_all, wr_pad, rb_pad, s_w_gate_t, s_w_up_t, s_w_down)


def _dispatch_body(starts_ref, idx_ref, rank_ref, u_ref, xs_hbm, slot_ref, sem, *, tokens):
    def issue(g, carry):
        base = pl.multiple_of(g * SUBLANES, SUBLANES)
        for j in range(SUBLANES):
            for kk in range(TOP_K):
                a = (base + j) * TOP_K + kk
                s = starts_ref[idx_ref[0, a]] + rank_ref[0, a]
                slot_ref[0, a] = s
                pltpu.make_async_copy(u_ref.at[g, pl.ds(j, 1), :], xs_hbm.at[pl.ds(s, 1), :], sem).start()
        return carry

    lax.fori_loop(0, tokens // SUBLANES, issue, 0)
    for _ in range(TOP_K):
        pltpu.make_async_copy(xs_hbm.at[pl.ds(0, tokens), :], xs_hbm.at[pl.ds(0, tokens), :], sem).wait()


def _dispatch(u, starts_ext, idx3, rank3, tokens):
    t, d = u.shape
    smem_blk = pl.BlockSpec((None, 1, tokens * TOP_K), lambda i, st: (i, 0, 0), memory_space=pltpu.SMEM)
    u = u.reshape(t // SUBLANES, SUBLANES, d)
    grid_spec = pltpu.PrefetchScalarGridSpec(
        num_scalar_prefetch=1,
        grid=(t // tokens,),
        in_specs=[smem_blk, smem_blk, pl.BlockSpec((tokens // SUBLANES, SUBLANES, d), lambda i, st: (i, 0, 0))],
        out_specs=(pl.BlockSpec(memory_space=pl.ANY), smem_blk),
        scratch_shapes=[pltpu.SemaphoreType.DMA],
    )
    return pl.pallas_call(
        functools.partial(_dispatch_body, tokens=tokens),
        out_shape=(jax.ShapeDtypeStruct((t * TOP_K, d), u.dtype), jax.ShapeDtypeStruct(idx3.shape, I32)),
        grid_spec=grid_spec,
        compiler_params=_cparams(("arbitrary",)),
        name="moe_dispatch",
    )(starts_ext, idx3, rank3, u)


_FFN_ROWS = 256


def _ffn_body(vt_ref, ve_ref, starts_ref, nvis_ref, xs_ref, wg_ref, wu_ref, wd_ref, ys_ref):
    v = pl.program_id(0)

    @pl.when(v < nvis_ref[0])
    def _():
        tile = vt_ref[v]
        e = ve_ref[v]
        x = _unpack_bf16_pairs(xs_ref[...])
        hg = _dot_nt(x, wg_ref[...].astype(BF16))
        hu = _dot_nt(x, wu_ref[...].astype(BF16))
        hh = (_silu(hg) * hu).astype(BF16)
        y = jnp.dot(hh, wd_ref[...].astype(BF16), preferred_element_type=F32)
        y = _pack_bf16_pairs(y.astype(BF16))
        first = jnp.logical_or(v == 0, vt_ref[jnp.maximum(v - 1, 0)] != tile)

        @pl.when(first)
        def _():
            ys_ref[...] = y

        @pl.when(jnp.logical_not(first))
        def _():
            r = lax.broadcasted_iota(I32, (_FFN_ROWS, 1), 0) + tile * _FFN_ROWS
            mine = jnp.logical_and(r >= starts_ref[e], r < starts_ref[e + 1])
            ys_ref[...] = jnp.where(mine, y, ys_ref[...])


def _grouped_ffn(xs, vt, ve, starts, nvis, e_w_gate_t, e_w_up_t, e_w_down, layer):
    ns = xs.shape[0]
    di, d = e_w_down.shape[2:]
    nv = vt.shape[0]
    w_blk = pl.BlockSpec((None, None, di, d), lambda v, vt, ve, st, nn: (layer, ve[v], 0, 0))
    grid_spec = pltpu.PrefetchScalarGridSpec(
        num_scalar_prefetch=4,
        grid=(nv,),
        in_specs=[pl.BlockSpec((_FFN_ROWS, d // 2), lambda v, vt, ve, st, nn: (vt[v], 0)), w_blk, w_blk, w_blk],
        out_specs=pl.BlockSpec((_FFN_ROWS, d // 2), lambda v, vt, ve, st, nn: (vt[v], 0)),
    )
    return pl.pallas_call(
        _ffn_body,
        out_shape=jax.ShapeDtypeStruct((ns, d // 2), U32),
        grid_spec=grid_spec,
        compiler_params=_cparams(("arbitrary",)),
        name="moe_grouped_ffn",
    )(vt, ve, starts, nvis, xs, e_w_gate_t, e_w_up_t, e_w_down)


_COMB_TOKENS = 128


_COMB_LANES = 1024


def _combine_body(*refs, row, d, final):
    refs = list(refs)
    slot_ref, next_slot_ref = refs.pop(0), refs.pop(0)
    meta_ref, xb_ref, mods_ref = refs.pop(0), refs.pop(0), refs.pop(0)
    fn_ref = refs.pop(0) if final else None
    ys_hbm, o_ref, gbuf_a, gbuf_b, sem_a, sem_b = refs
    half = d // 2
    lanes = min(_COMB_LANES, half)
    tc = _COMB_TOKENS
    i = pl.program_id(0)

    def issue_group(slots, slot_off, g, gbuf, sem):
        base = pl.multiple_of(g * SUBLANES, SUBLANES)
        for j in range(SUBLANES):
            for kk in range(TOP_K):
                s = slots[0, slot_off + (base + j) * TOP_K + kk]
                pltpu.make_async_copy(ys_hbm.at[pl.ds(s, 1), :], gbuf.at[g, kk, pl.ds(j, 1), :], sem).start()

    def wait_all(gbuf, sem):
        for kk in range(TOP_K):
            pltpu.make_async_copy(ys_hbm.at[pl.ds(0, tc), :], ys_hbm.at[pl.ds(0, tc), :], sem).wait()

    def rows(g, gbuf, row_off):
        ro =pl.ds(pl.multiple_of(row_off + g * SUBLANES, SUBLANES), SUBLANES)
        meta = meta_ref[ro, :]
        ssq = jnp.zeros((SUBLANES, 1), F32)
        for c0 in range(0, half, lanes):
            acc_lo = jnp.zeros((SUBLANES, lanes), F32)
            acc_hi = jnp.zeros((SUBLANES, lanes), F32)
            for kk in range(TOP_K):
                p = gbuf[g, kk, :, c0:c0 + lanes]
                w = meta[:, META_W + kk:META_W + kk + 1]
                acc_lo = acc_lo + w * lax.bitcast_convert_type(p << 16, F32)
                acc_hi = acc_hi + w * lax.bitcast_convert_type(p & jnp.uint32(_HI16), F32)
            for acc, col in ((acc_lo, c0), (acc_hi, half + c0)):
                cols = slice(col, col + lanes)
                gate = mods_ref[row:row + 1, G2 * d + col:G2 * d + col + lanes]
                out = xb_ref[ro, cols] + gate * acc
                o_ref[ro, cols] = out
                if final:
                    ssq = ssq + jnp.sum(out * out, axis=-1, keepdims=True)
        if final:
            o_ref[ro, :] = o_ref[ro, :] * lax.rsqrt(ssq * (1.0 / d) + EPS) * fn_ref[...]

    n_groups = tc // SUBLANES

    @pl.when(i == 0)
    def _():
        def first(g, carry):
            issue_group(slot_ref, 0, g, gbuf_a, sem_a)
            return carry
        lax.fori_loop(0, n_groups, first, 0)

    wait_all(gbuf_a, sem_a)

    def step_a(g, carry):
        issue_group(slot_ref, tc * TOP_K, g, gbuf_b, sem_b)
        rows(g, gbuf_a, 0)
        return carry
    lax.fori_loop(0, n_groups, step_a, 0)

    wait_all(gbuf_b, sem_b)

    def step_b(g, carry):
        issue_group(next_slot_ref, 0, g, gbuf_a, sem_a)
        rows(g, gbuf_b, tc)
        return carry
    lax.fori_loop(0, n_groups, step_b, 0)

    @pl.when(i == pl.num_programs(0) - 1)
    def _():
        wait_all(gbuf_a, sem_a)


def _combine(ys, slot3, meta, xb, mods_all, layer, row, final_norm=None):
    t, d = xb.shape
    tc = _COMB_TOKENS
    nb = t // (2 * tc)
    assert t % (2 * tc) == 0 and (d // 2) % min(_COMB_LANES, d // 2) == 0
    blk = pl.BlockSpec((2 * tc, d), lambda i: (i, 0))
    in_specs = [pl.BlockSpec((None, 1, 2 * tc * TOP_K), lambda i: (i, 0, 0), memory_space=pltpu.SMEM),
                pl.BlockSpec((None, 1, 2 * tc * TOP_K), lambda i: (jnp.minimum(i + 1, nb - 1), 0, 0),
                             memory_space=pltpu.SMEM),
                pl.BlockSpec((2 * tc, LANES), lambda i: (i, 0)), blk,
                pl.BlockSpec((None, SUBLANES, mods_all.shape[-1]), lambda i: (layer, 0, 0))]
    args = [slot3, slot3, meta, xb, mods_all]
    if final_norm is not None:
        in_specs.append(pl.BlockSpec((1, d), lambda i: (0, 0)))
        args.append(final_norm)
    in_specs.append(pl.BlockSpec(memory_space=pl.ANY))
    args.append(ys)
    return pl.pallas_call(
        functools.partial(_combine_body, row=row, d=d, final=final_norm is not None),
        out_shape=jax.ShapeDtypeStruct((t, d), F32),
        grid=(nb,),
        in_specs=in_specs,
        out_specs=blk,
        scratch_shapes=[pltpu.VMEM((tc // SUBLANES, TOP_K, SUBLANES, d // 2), U32),
                        pltpu.VMEM((tc // SUBLANES, TOP_K, SUBLANES, d // 2), U32),
                        pltpu.SemaphoreType.DMA, pltpu.SemaphoreType.DMA],
        compiler_params=_cparams(("arbitrary",)),
        name="moe_combine",
    )(*args)


def _moe_layer(x, layer, row, mods_all, norm_ffn3, wr_pad, rb_pad, e_w_gate_t, e_w_up_t, e_w_down,
               s_w_gate_t, s_w_up_t, s_w_down, final_norm):
    t, d = x.shape
    n_experts = e_w_down.shape[1]
    u, xb, meta, counts = _router(x, norm_ffn3, layer, mods_all, row, wr_pad, rb_pad,
                                  s_w_gate_t, s_w_up_t, s_w_down, n_experts)
    counts = counts[0, :n_experts].astype(I32)
    ends = jnp.cumsum(counts)
    starts = ends - counts
    idx = meta[:, META_IDX:META_IDX + TOP_K].astype(I32).reshape(-1)
    rank = meta[:, META_RANK:META_RANK + TOP_K].astype(I32).reshape(-1)
    ns = t * TOP_K
    n_tiles = ns // _FFN_ROWS
    first_tile = starts // _FFN_ROWS
    ntile_e = jnp.where(counts > 0, (ends - 1) // _FFN_ROWS - first_tile + 1, 0)
    vend = jnp.cumsum(ntile_e)
    vstart = vend - ntile_e
    nvis = vend[-1]
    nv = n_tiles + n_experts - 1
    v = jnp.minimum(jnp.arange(nv, dtype=I32), nvis - 1)
    ve = jnp.minimum(jnp.sum((vend[None, :] <= v[:, None]).astype(I32), axis=1), n_experts - 1)
    onehot = (ve[:, None] == jnp.arange(n_experts, dtype=I32)[None, :]).astype(I32)
    vt = (jnp.sum(onehot * (first_tile - vstart)[None, :], axis=1) + v).astype(I32)
    starts_ext = jnp.concatenate([starts, ends[-1:]]).astype(I32)

    disp_tokens = min(256, t)
    per = lambda a, n: a.reshape(t // n, 1, n * TOP_K)
    xs, slot3 = _dispatch(u, starts_ext, per(idx, disp_tokens), per(rank, disp_tokens), disp_tokens)
    ys = _grouped_ffn(xs, vt, ve, starts_ext, nvis.reshape(1).astype(I32), e_w_gate_t, e_w_up_t, e_w_down, layer)
    return _combine(ys, per(slot3, 2 * _COMB_TOKENS), meta, xb, mods_all, layer, row, final_norm)


def _pad_cols(a, n):
    return jnp.pad(a, ((0, 0), (0, n - a.shape[1])))


def _mlstm_layer(streams, mods, norm, a_w_in, j, gate_bias, head_norm3, a_w_out, ctx_out):
    H = M_HEADS
    d = a_w_in.shape[1]
    dv = head_norm3.shape[-1] // H
    dk = dv // 2
    n_proj = 2 * H * dk + 2 * H * dv
    a_w_in_t = jnp.swapaxes(a_w_in, 1, 2)
    w_gates = jnp.pad(a_w_in_t[j, n_proj:], ((0, LANES - 4 * H), (0, 0)))
    w_in_t = a_w_in_t.astype(BF16)
    w_out = a_w_out[j].astype(BF16)
    bias_row = _pad_cols(gate_bias.reshape(1, 4 * H), LANES)
    bias_col = bias_row.reshape(LANES, 1)

    projs = []
    for xs, row in streams:
        proj, gates = _fused_matmul(xs, w_in_t, w_t_layer=j, n_out=n_proj, prologue="mod", norm=norm, mods=mods, row=row,
                                    extra_w=w_gates, out_dtype=BF16, name="mlstm_in_proj")
        projs.append((proj, gates, gates.T))

    zero = (jnp.zeros((H, dk, dv), F32), jnp.zeros((H, 1, dk), F32), jnp.zeros((H, 1, LANES), F32))
    (px, gx, gxt), (pc, gc, gct) = projs
    hn = (head_norm3, j)
    hcf, st_f = _mlstm_scan(pc, gc, gct, bias_row, bias_col, zero, reverse=False)
    hxf, _ = _mlstm_scan(px, gx, gxt, bias_row, bias_col, st_f, reverse=False)
    yc_pre, st_b = _mlstm_scan(pc, gc, gct, bias_row, bias_col, zero, reverse=True, finalize=(hcf, hn))
    yx_pre, _ = _mlstm_scan(px, gx, gxt, bias_row, bias_col, st_b, reverse=True, finalize=(hxf, hn))

    outs = []
    for (xs, row), y_pre, live in zip(streams, (yx_pre, yc_pre), (True, ctx_out)):
        if live:
            outs.append(_fused_matmul(y_pre, w_out, n_out=d, epilogue="residual", res=xs, mods=mods,
                                      row=row, name="mlstm_out_proj"))
        else:
            outs.append(xs)
    return outs


def _conv_layer(streams, mods, norm, b_w_in, b_w_dw, ln, b_w_out, j, live_flags, seq_lens):
    d = b_w_in.shape[1]
    w_in = b_w_in[j].astype(BF16)
    w_out = b_w_out[j].astype(BF16)
    outs = []
    for (xs, row), live, seq_len in zip(streams, live_flags, seq_lens):
        if not live:
            outs.append(xs)
            continue
        hglu = _fused_matmul(xs, w_in, n_out=d, prologue="mod", norm=norm, mods=mods, row=row,
                             epilogue="glu", w2_col_offset=d, name="conv_in_glu")
        hc = _dwconv(hglu, b_w_dw, j, seq_len=seq_len)
        outs.append(_fused_matmul(hc, w_out, n_out=d, prologue="ln_silu", ln=(ln, j),
                                  epilogue="residual", res=xs, mods=mods, row=row, name="conv_out_proj"))
    return outs


def kernel(x, c, ctx, c_ctx, ada_down, ada_up, ada_bias, norm_mix, norm_ffn, a_w_in, a_gate_bias, a_head_norm, a_w_out, b_w_in, b_w_dw, b_ln_g, b_ln_b, b_w_out, w_router, router_bias, e_w_gate, e_w_up, e_w_down, s_w_gate, s_w_up, s_w_down, final_norm):
    bsz, t, d = x.shape
    assert bsz == 1 and c.shape[0] == 1 and ctx.shape[0] == 1
    depth = ada_down.shape[0]
    n_experts = w_router.shape[-1]
    t_ctx = ctx.shape[1]

    cond = jnp.zeros((SUBLANES, d), F32).at[0].set(c[0]).at[1].set(c_ctx)
    mods_all = _ada_all(cond, ada_down, ada_up, ada_bias)
    norm_mix3 = norm_mix.reshape(depth, 1, d)
    norm_ffn3 = norm_ffn.reshape(depth, 1, d)
    head_norm3 = a_head_norm.reshape(a_head_norm.shape[0], 1, -1)
    ln3 = (b_ln_g.reshape(-1, 1, d), b_ln_b.reshape(-1, 1, d))
    fn = final_norm.reshape(1, d)
    e_gate_t, e_up_t = jnp.swapaxes(e_w_gate, 2, 3), jnp.swapaxes(e_w_up, 2, 3)
    s_gate_t, s_up_t = jnp.swapaxes(s_w_gate, 1, 2), jnp.swapaxes(s_w_up, 1, 2)

    xs, cs = x[0], ctx[0]
    for i in range(depth):
        kind, j = i % 2, i // 2
        ctx_live = any(l % 2 == 0 for l in range(i + 1, depth))
        mods = (mods_all, i)
        norm = (norm_mix3, i)
        streams = [(xs, 0), (cs, 1)]
        if kind == 0:
            xs, cs = _mlstm_layer(streams, mods, norm, a_w_in, j, a_gate_bias[j], head_norm3, a_w_out, ctx_live)
        else:
            xs, cs = _conv_layer(streams, mods, norm, b_w_in, b_w_dw, ln3, b_w_out, j,
                                 (True, ctx_live), (GRID_W, t_ctx))

        wr = jnp.pad(w_router[i].T, ((0, LANES - n_experts), (0, 0)))
        rb = _pad_cols(router_bias[i].reshape(1, n_experts), LANES)
        last = i == depth - 1
        new = []
        for (s, row), live in zip(((xs, 0), (cs, 1)), (True, ctx_live)):
            if live:
                s = _moe_layer(s, i, row, mods_all, norm_ffn3, wr, rb, e_gate_t, e_up_t, e_w_down,
                               s_gate_t, s_up_t, s_w_down, fn if (last and row == 0) else None)
            new.append(s)
        xs, cs = new
    return xs[None]
```

```python
import functools

import jax
import jax.numpy as jnp
from jax import lax
from jax.experimental import pallas as pl
from jax.experimental.pallas import tpu as pltpu

F32 = jnp.float32
BF16 = jnp.bfloat16
I32 = jnp.int32
U32 = jnp.uint32

GRID_W = 64
M_HEADS = 8
M_CHUNK = 128
CONV_W = 31
CONV_HALF = CONV_W // 2
TOP_K = 6
ROUTED_SCALE = 2.5
N_MOD = 6
EPS = 1e-6
SH1, SC1, G1, SH2, SC2, G2 = range(N_MOD)

LANES = 128
SUBLANES = 8
VMEM_LIMIT = 56 * 1024 * 1024
HIGHEST = lax.Precision.HIGHEST


def _cparams(sem):
    return pltpu.CompilerParams(dimension_semantics=sem, vmem_limit_bytes=VMEM_LIMIT)


def _sigmoid(v):
    return 1.0 / (1.0 + jnp.exp(-v))


def _silu(v):
    return v * _sigmoid(v)


def _log_sigmoid(v):
    return jnp.minimum(v, 0.0) - jnp.log(1.0 + jnp.exp(-jnp.abs(v)))


def _dot_nt(a, b_t):
    return lax.dot_general(a, b_t, (((1,), (1,)), ((), ())), preferred_element_type=F32)


_HI16 = 0xFFFF0000


def _pack_bf16_pairs(u):
    half = u.shape[1] // 2
    bits = lax.bitcast_convert_type(u.astype(F32), U32)
    return (bits[:, half:] & jnp.uint32(_HI16)) | (bits[:, :half] >> 16)


def _unpack_bf16_pairs(p):
    lo = lax.bitcast_convert_type(p << 16, F32)
    hi = lax.bitcast_convert_type(p & jnp.uint32(_HI16), F32)
    return jnp.concatenate([lo, hi], axis=1).astype(BF16)


def _rmsnorm(x, g):
    return x * lax.rsqrt(jnp.mean(x * x, axis=-1, keepdims=True) + EPS) * g


def _ada_body(cond_ref, down_ref, up_ref, bias_ref, o_ref):
    s = _silu(cond_ref[...])
    t = jnp.dot(s, down_ref[...], precision=HIGHEST, preferred_element_type=F32)
    o_ref[...] = jnp.dot(t, up_ref[...], precision=HIGHEST, preferred_element_type=F32) + bias_ref[...]


def _ada_all(cond, ada_down, ada_up, ada_bias):
    n_layers, d, rank = ada_down.shape
    n_out = ada_up.shape[-1]
    tn = d
    assert n_out % tn == 0
    return pl.pallas_call(
        _ada_body,
        out_shape=jax.ShapeDtypeStruct((n_layers, SUBLANES, n_out), F32),
        grid=(n_layers, n_out // tn),
        in_specs=[
            pl.BlockSpec((SUBLANES, d), lambda l, j: (0, 0)),
            pl.BlockSpec((None, d, rank), lambda l, j: (l, 0, 0)),
            pl.BlockSpec((None, rank, tn), lambda l, j: (l, 0, j)),
            pl.BlockSpec((None, 1, tn), lambda l, j: (l, 0, j)),
        ],
        out_specs=pl.BlockSpec((None, SUBLANES, tn), lambda l, j: (l, 0, j)),
        compiler_params=_cparams(("parallel", "arbitrary")),
        name="ada_mod",
    )(cond, ada_down, ada_up, ada_bias.reshape(n_layers, 1, n_out))


_PRO_ROWS = 32


def _mm_body(*refs, prologue, epilogue, extra, tm, row, d_mod, w_transposed, pro_rows):
    refs = list(refs)
    a_ref = refs.pop(0)
    if prologue == "mod":
        g_ref, mods_ref = refs.pop(0), refs.pop(0)
    elif prologue == "ln_silu":
        g_ref, b_ref = refs.pop(0), refs.pop(0)
    w_ref = refs.pop(0)
    if epilogue == "glu":
        w2_ref = refs.pop(0)
    elif epilogue == "residual":
        res_ref, gate_ref = refs.pop(0), refs.pop(0)
    if extra:
        wx_ref = refs.pop(0)
    o_ref = refs.pop(0)
    if extra:
        ox_ref = refs.pop(0)

    if prologue == "none":
        u = a_ref[...]
    else:
        u_ref = refs.pop(0)

        @pl.when(pl.program_id(1) == 0)
        def _():
            def rows(r, carry):
                sl = pl.ds(pl.multiple_of(r * pro_rows, pro_rows), pro_rows)
                x = a_ref[sl, :]
                if prologue == "mod":
                    shift = mods_ref[row:row + 1, SH1 * d_mod:(SH1 + 1) * d_mod]
                    scale = mods_ref[row:row + 1, SC1 * d_mod:(SC1 + 1) * d_mod]
                    y = _rmsnorm(x, g_ref[...]) * (1.0 + scale) + shift
                else:
                    mu = jnp.mean(x, axis=-1, keepdims=True)
                    xc = x - mu
                    var = jnp.mean(xc * xc, axis=-1, keepdims=True)
                    y = _silu(xc * lax.rsqrt(var + EPS) * g_ref[...] + b_ref[...])
                u_ref[sl, :] = y.astype(BF16)
                return carry

            lax.fori_loop(0, tm // pro_rows, rows, 0)

        u = u_ref[...]

    if extra:
        @pl.when(pl.program_id(1) == 0)
        def _():
            ox_ref[...] = _dot_nt(u, wx_ref[...].astype(BF16))

    if w_transposed:
        acc = _dot_nt(u, w_ref[...])
    else:
        acc = jnp.dot(u, w_ref[...], preferred_element_type=F32)
    if epilogue == "glu":
        acc2 = jnp.dot(u, w2_ref[...], preferred_element_type=F32)
        acc = acc * _sigmoid(acc2)
    elif epilogue == "residual":
        acc = res_ref[...] + gate_ref[row:row + 1, :] * acc
    o_ref[...] = acc.astype(o_ref.dtype)


def _fused_matmul(a, w, *, n_out, prologue="none", norm=None, ln=None, mods=None, row=0,
                  epilogue="plain", res=None, w2_col_offset=0, extra_w=None, w_t_layer=None,
                  out_dtype=F32, tm=1024, tn=512, pro_rows=_PRO_ROWS, name="mm"):
    m, k = a.shape
    tm = min(tm, m)
    tn = min(tn, n_out)
    assert m % tm == 0 and n_out % tn == 0 and w.dtype == BF16 and tm % pro_rows == 0
    assert w.shape[-1 if w_t_layer is not None else 0] == k
    d_mod = None
    a_mode = {} if prologue == "none" else dict(pipeline_mode=pl.Buffered(1))
    in_specs = [pl.BlockSpec((tm, k), lambda i, j: (i, 0), **a_mode)]
    args = [a]
    if prologue == "mod":
        norm_arr, norm_layer = norm
        mods_all, mods_layer = mods
        d_mod = mods_all.shape[-1] // N_MOD
        in_specs += [pl.BlockSpec((None, 1, k), lambda i, j: (norm_layer, 0, 0)),
                     pl.BlockSpec((None, SUBLANES, mods_all.shape[-1]), lambda i, j: (mods_layer, 0, 0))]
        args += [norm_arr, mods_all]
    elif prologue == "ln_silu":
        (ln_g, ln_b), ln_layer = ln
        in_specs += [pl.BlockSpec((None, 1, k), lambda i, j: (ln_layer, 0, 0))] * 2
        args += [ln_g, ln_b]
    if w_t_layer is not None:
        in_specs.append(pl.BlockSpec((None, tn, k), lambda i, j: (w_t_layer, j, 0)))
    else:
        in_specs.append(pl.BlockSpec((k, tn), lambda i, j: (0, j)))
    args.append(w)
    if epilogue == "glu":
        off = w2_col_offset // tn
        in_specs.append(pl.BlockSpec((k, tn), lambda i, j: (0, j + off)))
        args.append(w)
    elif epilogue == "residual":
        mods_all, mods_layer = mods
        goff = G1 * (mods_all.shape[-1] // N_MOD) // tn
        in_specs += [pl.BlockSpec((tm, tn), lambda i, j: (i, j)),
                     pl.BlockSpec((None, SUBLANES, tn), lambda i, j: (mods_layer, 0, goff + j))]
        args += [res, mods_all]
    out_shape = [jax.ShapeDtypeStruct((m, n_out), out_dtype)]
    out_specs = [pl.BlockSpec((tm, tn), lambda i, j: (i, j))]
    if extra_w is not None:
        in_specs.append(pl.BlockSpec((LANES, k), lambda i, j: (0, 0)))
        args.append(extra_w)
        out_shape.append(jax.ShapeDtypeStruct((m, LANES), F32))
        out_specs.append(pl.BlockSpec((tm, LANES), lambda i, j: (i, 0)))
    scratch = [] if prologue == "none" else [pltpu.VMEM((tm, k), BF16)]
    outs = pl.pallas_call(
        functools.partial(_mm_body, prologue=prologue, epilogue=epilogue, extra=extra_w is not None,
                          w_transposed=w_t_layer is not None,
                          tm=tm, row=row, d_mod=d_mod, pro_rows=pro_rows),
        out_shape=out_shape,
        grid=(m // tm, n_out // tn),
        in_specs=in_specs,
        out_specs=out_specs,
        scratch_shapes=scratch,
        compiler_params=_cparams(("parallel", "arbitrary")),
        name=name,
    )(*args)
    return outs if extra_w is not None else outs[0]


def _scan_body(*refs, reverse, finalize, dk, dv):
    refs = list(refs)
    q_ref, k_ref, v_ref = refs.pop(0), refs.pop(0), refs.pop(0)
    gcol_ref, grow_ref, bias_row_ref, bias_col_ref = refs.pop(0), refs.pop(0), refs.pop(0), refs.pop(0)
    c0_ref, n0_ref, m0_ref = refs.pop(0), refs.pop(0), refs.pop(0)
    if finalize:
        o_ref, hprev_ref, hnorm_ref = refs.pop(0), refs.pop(0), refs.pop(0)
    out_ref, c_ref, n_ref, m_ref = refs.pop(0), refs.pop(0), refs.pop(0), refs.pop(0)

    L = M_CHUNK
    H = M_HEADS

    @pl.when(pl.program_id(0) == 0)
    def _():
        c_ref[...] = c0_ref[...]
        n_ref[...] = n0_ref[...]
        m_ref[...] = m0_ref[...]

    kind_i = 2 if reverse else 0
    kind_f = kind_i + 1

    t_idx = lax.broadcasted_iota(I32, (L, L), 0)
    s_idx = lax.broadcasted_iota(I32, (L, L), 1)
    mask = (s_idx >= t_idx) if reverse else (s_idx <= t_idx)
    mask_f = mask.astype(F32)
    mask_t = ((t_idx >= s_idx) if reverse else (t_idx <= s_idx)).astype(F32)
    q_scale = dk ** -0.5

    gcol = gcol_ref[...] + bias_row_ref[...]
    grow = grow_ref[...] + bias_col_ref[...]
    lf_col = _log_sigmoid(gcol)
    lf_row = _log_sigmoid(grow[kind_f * H:(kind_f + 1) * H, :])
    b_col_all = jnp.dot(mask_f, lf_col, precision=HIGHEST, preferred_element_type=F32)
    b_row_all = jnp.dot(lf_row, mask_t, precision=HIGHEST, preferred_element_type=F32)

    qs = lambda h: q_ref[:, h * dk:(h + 1) * dk]
    ks = lambda h: k_ref[:, h * dk:(h + 1) * dk]
    vs = lambda h: v_ref[:, h * dv:(h + 1) * dv]

    s_raw = [_dot_nt(qs(h), ks(h)) for h in range(H)]
    qc = [jnp.dot(qs(h), c_ref[h].astype(BF16), preferred_element_type=F32) for h in range(H)]

    st = []
    for h in range(H):
        ci, cf = kind_i * H + h, kind_f * H + h
        b_col = b_col_all[:, cf:cf + 1]
        b_row = b_row_all[h:h + 1, :]
        i_col = gcol[:, ci:ci + 1]
        i_row = grow[ci:ci + 1, :]
        m_prev = m_ref[h][:, :1]
        dmat = jnp.where(mask, b_col - b_row + i_row, -jnp.inf)
        m_inter = b_col + m_prev
        m_t = jnp.maximum(m_inter, jnp.max(dmat, axis=-1, keepdims=True))
        w_intra = jnp.exp(dmat - m_t) * q_scale
        w_inter = jnp.exp(m_inter - m_t) * q_scale
        b_last = jnp.sum(lf_row[h:h + 1, :], axis=-1, keepdims=True)
        m_new = jnp.maximum(b_last + m_prev, jnp.max(b_last - b_row + i_row, axis=-1, keepdims=True))
        kw = ks(h).astype(F32) * jnp.exp(b_last - b_col + i_col - m_new)
        decay = jnp.exp(b_last + m_prev - m_new)
        s = s_raw[h] * w_intra
        st.append(dict(m_t=m_t, w_inter=w_inter, m_new=m_new, kw=kw, decay=decay, s=s))

    sv = [jnp.dot(st[h]["s"].astype(BF16), vs(h), preferred_element_type=F32) for h in range(H)]
    kv = [lax.dot_general(st[h]["kw"].astype(BF16), vs(h), (((0,), (0,)), ((), ())),
                          preferred_element_type=F32) for h in range(H)]

    for h in range(H):
        a = st[h]
        n_state = n_ref[h]
        num = sv[h] + a["w_inter"] * qc[h]
        qn = jnp.sum(qs(h).astype(F32) * n_state, axis=-1, keepdims=True)
        den = jnp.sum(a["s"], axis=-1, keepdims=True) + a["w_inter"] * qn
        hv = num * (1.0 / jnp.maximum(jnp.abs(den), jnp.exp(-a["m_t"])))

        c_ref[h] = a["decay"] * c_ref[h] + kv[h]
        n_ref[h] = a["decay"] * n_state + jnp.sum(a["kw"], axis=0, keepdims=True)
        m_ref[h] = jnp.broadcast_to(a["m_new"], (1, LANES))

        cols = slice(h * dv, (h + 1) * dv)
        if finalize:
            ht = hv + hprev_ref[:, cols]
            ht = ht * lax.rsqrt(jnp.mean(ht * ht, axis=-1, keepdims=True) + EPS)
            ht = ht * hnorm_ref[:, cols] * _sigmoid(o_ref[:, cols].astype(F32))
            out_ref[:, cols] = ht.astype(out_ref.dtype)
        else:
            out_ref[:, cols] = hv


def _mlstm_scan(proj, gcol, grow, bias_row, bias_col, state, *, reverse, finalize=None):
    t = proj.shape[0]
    H, L = M_HEADS, M_CHUNK
    c0, n0, m0 = state
    dk, dv = c0.shape[1], c0.shape[2]
    nc = t // L
    cidx = (lambda i: nc - 1 - i) if reverse else (lambda i: i)
    qk_w, v_w = H * dk, H * dv
    assert 2 * qk_w == v_w
    full = lambda shape: pl.BlockSpec(shape, lambda i: (0,) * len(shape))
    in_specs = [
        pl.BlockSpec((L, qk_w), lambda i: (cidx(i), 0)),
        pl.BlockSpec((L, qk_w), lambda i: (cidx(i), 1)),
        pl.BlockSpec((L, v_w), lambda i: (cidx(i), 1)),
        pl.BlockSpec((L, LANES), lambda i: (cidx(i), 0)),
        pl.BlockSpec((LANES, L), lambda i: (0, cidx(i))),
        full((1, LANES)), full((LANES, 1)),
        full(c0.shape), full(n0.shape), full(m0.shape),
    ]
    args = [proj, proj, proj, gcol, grow, bias_row, bias_col, c0, n0, m0]
    if finalize is not None:
        hprev, (hnorm, hn_layer) = finalize
        in_specs += [pl.BlockSpec((L, v_w), lambda i: (cidx(i), 2)),
                     pl.BlockSpec((L, v_w), lambda i: (cidx(i), 0)),
                     pl.BlockSpec((None, 1, v_w), lambda i: (hn_layer, 0, 0))]
        args += [proj, hprev, hnorm]
    out_dtype = BF16 if finalize is not None else F32
    out, c1, n1, m1 = pl.pallas_call(
        functools.partial(_scan_body, reverse=reverse, finalize=finalize is not None, dk=dk, dv=dv),
        out_shape=(jax.ShapeDtypeStruct((t, v_w), out_dtype),
                   jax.ShapeDtypeStruct(c0.shape, F32),
                   jax.ShapeDtypeStruct(n0.shape, F32),
                   jax.ShapeDtypeStruct(m0.shape, F32)),
        grid=(nc,),
        in_specs=in_specs,
        out_specs=(pl.BlockSpec((L, v_w), lambda i: (cidx(i), 0)),
                   full(c0.shape), full(n0.shape), full(m0.shape)),
        compiler_params=_cparams(("arbitrary",)),
        name="mlstm_scan_bwd" if reverse else "mlstm_scan_fwd",
    )(*args)
    return out, (c1, n1, m1)


_CONV_LANES = 512
_CONV_ROWS = 64
_CONV_PAD = 16


def _conv_body(x_ref, w_ref, o_ref, pad_ref, shift_ref, *, seq_len, n_seq):
    zeros = jnp.zeros((_CONV_PAD, _CONV_LANES), F32)
    pad_ref[0:_CONV_PAD, :] = zeros
    pad_ref[_CONV_PAD + seq_len:2 * _CONV_PAD + seq_len, :] = zeros
    n_shift_rows = shift_ref.shape[1]

    def one_seq(s, carry):
        base = pl.multiple_of(s * seq_len, seq_len)
        pad_ref[_CONV_PAD:_CONV_PAD + seq_len, :] = x_ref[pl.ds(base, seq_len), :]
        for b in range(1, SUBLANES):
            shift_ref[b] = pad_ref[b:b + n_shift_rows, :]
        for r0 in range(0, seq_len, _CONV_ROWS):
            acc = jnp.zeros((_CONV_ROWS, _CONV_LANES), F32)
            for j in range(CONV_W):
                start = _CONV_PAD - CONV_HALF + j + r0
                b, a = start % SUBLANES, start - start % SUBLANES
                src = pad_ref if b == 0 else shift_ref.at[b]
                acc = acc + w_ref[j:j + 1, :] * src[a:a + _CONV_ROWS, :]
            o_ref[pl.ds(base + r0, _CONV_ROWS), :] = acc
        return carry

    lax.fori_loop(0, n_seq, one_seq, 0)


def _dwconv(x, w_dw, layer, *, seq_len):
    t, d = x.shape
    n_seq = max(1, min(512, t) // seq_len)
    rows = n_seq * seq_len
    assert t % rows == 0 and d % _CONV_LANES == 0 and seq_len % _CONV_ROWS == 0
    return pl.pallas_call(
        functools.partial(_conv_body, seq_len=seq_len, n_seq=n_seq),
        out_shape=jax.ShapeDtypeStruct((t, d), F32),
        grid=(t // rows, d // _CONV_LANES),
        in_specs=[pl.BlockSpec((rows, _CONV_LANES), lambda i, j: (i, j)),
                  pl.BlockSpec((None, CONV_W, _CONV_LANES), lambda i, j: (layer, 0, j))],
        out_specs=pl.BlockSpec((rows, _CONV_LANES), lambda i, j: (i, j)),
        scratch_shapes=[pltpu.VMEM((seq_len + 2 * _CONV_PAD, _CONV_LANES), F32),
                        pltpu.VMEM((SUBLANES, seq_len + 2 * _CONV_PAD - SUBLANES, _CONV_LANES), F32)],
        compiler_params=_cparams(("parallel", "parallel")),
        name="dwconv",
    )(x, w_dw)


META_IDX, META_RANK, META_W = 0, 8, 16


def _router_body(x_ref, g_ref, mods_ref, wr_ref, rb_ref, swg_ref, swu_ref, swd_ref,
                 u_ref, xb_ref, meta_ref, counts_ref, carry_ref, *, n_experts, row, d, tm):
    @pl.when(pl.program_id(0) == 0)
    def _():
        carry_ref[...] = jnp.zeros_like(carry_ref)

    x = x_ref[...]
    shift = mods_ref[row:row + 1, SH2 * d:(SH2 + 1) * d]
    scale = mods_ref[row:row + 1, SC2 * d:(SC2 + 1) * d]
    gate2 = mods_ref[row:row + 1, G2 * d:(G2 + 1) * d]
    uf = _rmsnorm(x, g_ref[...]) * (1.0 + scale) + shift
    u = uf.astype(BF16)
    u_ref[...] = _pack_bf16_pairs(u)

    hs = _silu(_dot_nt(u, swg_ref[...].astype(BF16)))
    hs = hs * _dot_nt(u, swu_ref[...].astype(BF16))
    shared = jnp.dot(hs.astype(BF16), swd_ref[...].astype(BF16), preferred_element_type=F32)
    xb_ref[...] = x + gate2 * shared

    logits = _dot_nt(u, wr_ref[...].astype(BF16))
    scores = _sigmoid(logits)
    lane = lax.broadcasted_iota(I32, scores.shape, 1)
    sel = jnp.where(lane < n_experts, scores + rb_ref[...], -jnp.inf)
    picked = jnp.zeros_like(scores)
    chosen = jnp.zeros_like(scores)
    firsts = []
    for _ in range(TOP_K):
        mx = jnp.max(sel, axis=-1, keepdims=True)
        first = jnp.min(jnp.where(sel == mx, lane, LANES), axis=-1, keepdims=True)
        hit = lane == first
        picked = jnp.where(hit, scores, picked)
        chosen = jnp.where(hit, 1.0, chosen)
        sel = jnp.where(hit, -jnp.inf, sel)
        firsts.append(first)
    gates = picked / jnp.sum(picked, axis=-1, keepdims=True) * ROUTED_SCALE

    r_idx = lax.broadcasted_iota(I32, (tm, tm), 0)
    c_idx = lax.broadcasted_iota(I32, (tm, tm), 1)
    before = (c_idx < r_idx).astype(BF16)
    pos = jnp.dot(before, chosen.astype(BF16), preferred_element_type=F32) + carry_ref[...]
    meta = jnp.zeros_like(scores)
    for kk, first in enumerate(firsts):
        hit = lane == first
        rank = jnp.sum(jnp.where(hit, pos, 0.0), axis=-1, keepdims=True)
        wk = jnp.sum(jnp.where(hit, gates, 0.0), axis=-1, keepdims=True)
        meta = jnp.where(lane == META_IDX + kk, first.astype(F32), meta)
        meta = jnp.where(lane == META_RANK + kk, rank, meta)
        meta = jnp.where(lane == META_W + kk, wk, meta)
    meta_ref[...] = meta
    carry_ref[...] += jnp.sum(chosen, axis=0, keepdims=True)
    counts_ref[...] = carry_ref[...]


def _router(x, norm_ffn3, layer, mods_all, row, wr_pad, rb_pad, s_w_gate_t, s_w_up_t, s_w_down, n_experts, tm=256):
    t, d = x.shape
    tm = min(tm, t)
    ds_ = s_w_gate_t.shape[1]
    c2 = lambda i: (0, 0)
    lay3 = lambda i: (layer, 0, 0)
    blk = pl.BlockSpec((tm, d), lambda i: (i, 0))
    return pl.pallas_call(
        functools.partial(_router_body, n_experts=n_experts, row=row, d=d, tm=tm),
        out_shape=(jax.ShapeDtypeStruct((t, d // 2), U32), jax.ShapeDtypeStruct((t, d), F32),
                   jax.ShapeDtypeStruct((t, LANES), F32), jax.ShapeDtypeStruct((1, LANES), F32)),
        grid=(t // tm,),
        in_specs=[blk,
                  pl.BlockSpec((None, 1, d), lay3),
                  pl.BlockSpec((None, SUBLANES, mods_all.shape[-1]), lay3),
                  pl.BlockSpec((LANES, d), c2), pl.BlockSpec((1, LANES), c2),
                  pl.BlockSpec((None, ds_, d), lay3), pl.BlockSpec((None, ds_, d), lay3),
                  pl.BlockSpec((None, ds_, d), lay3)],
        out_specs=(pl.BlockSpec((tm, d // 2), lambda i: (i, 0)), blk,
                   pl.BlockSpec((tm, LANES), lambda i: (i, 0)), pl.BlockSpec((1, LANES), c2)),
        scratch_shapes=[pltpu.VMEM((1, LANES), F32)],
        compiler_params=_cparams(("arbitrary",)),
        name="moe_router",
    )(x, norm_ffn3, mods_all, wr_pad, rb_pad, s_w_gate_t, s_w_up_t, s_w_down)


def _dispatch_body(starts_ref, idx_ref, rank_ref, u_ref, xs_hbm, slot_ref, sem, *, tokens):
    def issue(g, carry):
        base = pl.multiple_of(g * SUBLANES, SUBLANES)
        for j in range(SUBLANES):
            for kk in range(TOP_K):
                a = (base + j) * TOP_K + kk
                s = starts_ref[idx_ref[0, a]] + rank_ref[0, a]
                slot_ref[0, a] = s
                pltpu.make_async_copy(u_ref.at[g, pl.ds(j, 1), :], xs_hbm.at[pl.ds(s, 1), :], sem).start()
        return carry

    lax.fori_loop(0, tokens // SUBLANES, issue, 0)
    for _ in range(TOP_K):
        pltpu.make_async_copy(xs_hbm.at[pl.ds(0, tokens), :], xs_hbm.at[pl.ds(0, tokens), :], sem).wait()


def _dispatch(u, starts_ext, idx3, rank3, tokens):
    t, d = u.shape
    smem_blk = pl.BlockSpec((None, 1, tokens * TOP_K), lambda i, st: (i, 0, 0), memory_space=pltpu.SMEM)
    u = u.reshape(t // SUBLANES, SUBLANES, d)
    grid_spec = pltpu.PrefetchScalarGridSpec(
        num_scalar_prefetch=1,
        grid=(t // tokens,),
        in_specs=[smem_blk, smem_blk, pl.BlockSpec((tokens // SUBLANES, SUBLANES, d), lambda i, st: (i, 0, 0))],
        out_specs=(pl.BlockSpec(memory_space=pl.ANY), smem_blk),
        scratch_shapes=[pltpu.SemaphoreType.DMA],
    )
    return pl.pallas_call(
        functools.partial(_dispatch_body, tokens=tokens),
        out_shape=(jax.ShapeDtypeStruct((t * TOP_K, d), u.dtype), jax.ShapeDtypeStruct(idx3.shape, I32)),
        grid_spec=grid_spec,
        compiler_params=_cparams(("arbitrary",)),
        name="moe_dispatch",
    )(starts_ext, idx3, rank3, u)


_FFN_ROWS = 256


def _ffn_body(vt_ref, ve_ref, starts_ref, nvis_ref, xs_ref, wg_ref, wu_ref, wd_ref, ys_ref):
    v = pl.program_id(0)

    @pl.when(v < nvis_ref[0])
    def _():
        tile = vt_ref[v]
        e = ve_ref[v]
        x = _unpack_bf16_pairs(xs_ref[...])
        hg = _dot_nt(x, wg_ref[...].astype(BF16))
        hu = _dot_nt(x, wu_ref[...].astype(BF16))
        hh = (_silu(hg) * hu).astype(BF16)
        y = jnp.dot(hh, wd_ref[...].astype(BF16), preferred_element_type=F32)
        y = _pack_bf16_pairs(y.astype(BF16))
        first = jnp.logical_or(v == 0, vt_ref[jnp.maximum(v - 1, 0)] != tile)

        @pl.when(first)
        def _():
            ys_ref[...] = y

        @pl.when(jnp.logical_not(first))
        def _():
            r = lax.broadcasted_iota(I32, (_FFN_ROWS, 1), 0) + tile * _FFN_ROWS
            mine = jnp.logical_and(r >= starts_ref[e], r < starts_ref[e + 1])
            ys_ref[...] = jnp.where(mine, y, ys_ref[...])


def _grouped_ffn(xs, vt, ve, starts, nvis, e_w_gate_t, e_w_up_t, e_w_down, layer):
    ns = xs.shape[0]
    di, d = e_w_down.shape[2:]
    nv = vt.shape[0]
    w_blk = pl.BlockSpec((None, None, di, d), lambda v, vt, ve, st, nn: (layer, ve[v], 0, 0))
    grid_spec = pltpu.PrefetchScalarGridSpec(
        num_scalar_prefetch=4,
        grid=(nv,),
        in_specs=[pl.BlockSpec((_FFN_ROWS, d // 2), lambda v, vt, ve, st, nn: (vt[v], 0)), w_blk, w_blk, w_blk],
        out_specs=pl.BlockSpec((_FFN_ROWS, d // 2), lambda v, vt, ve, st, nn: (vt[v], 0)),
    )
    return pl.pallas_call(
        _ffn_body,
        out_shape=jax.ShapeDtypeStruct((ns, d // 2), U32),
        grid_spec=grid_spec,
        compiler_params=_cparams(("arbitrary",)),
        name="moe_grouped_ffn",
    )(vt, ve, starts, nvis, xs, e_w_gate_t, e_w_up_t, e_w_down)


_COMB_TOKENS = 128


_COMB_LANES = 1024


def _combine_body(*refs, row, d, final):
    refs = list(refs)
    slot_ref, next_slot_ref = refs.pop(0), refs.pop(0)
    meta_ref, xb_ref, mods_ref = refs.pop(0), refs.pop(0), refs.pop(0)
    fn_ref = refs.pop(0) if final else None
    ys_hbm, o_ref, gbuf_a, gbuf_b, sem_a, sem_b = refs
    half = d // 2
    lanes = min(_COMB_LANES, half)
    tc = _COMB_TOKENS
    i = pl.program_id(0)

    def issue_group(slots, slot_off, g, gbuf, sem):
        base = pl.multiple_of(g * SUBLANES, SUBLANES)
        for j in range(SUBLANES):
            for kk in range(TOP_K):
                s = slots[0, slot_off + (base + j) * TOP_K + kk]
                pltpu.make_async_copy(ys_hbm.at[pl.ds(s, 1), :], gbuf.at[g, kk, pl.ds(j, 1), :], sem).start()

    def wait_all(gbuf, sem):
        for kk in range(TOP_K):
            pltpu.make_async_copy(ys_hbm.at[pl.ds(0, tc), :], ys_hbm.at[pl.ds(0, tc), :], sem).wait()

    def rows(g, gbuf, row_off):
        ro =pl.ds(pl.multiple_of(row_off + g * SUBLANES, SUBLANES), SUBLANES)
        meta = meta_ref[ro, :]
        ssq = jnp.zeros((SUBLANES, 1), F32)
        for c0 in range(0, half, lanes):
            acc_lo = jnp.zeros((SUBLANES, lanes), F32)
            acc_hi = jnp.zeros((SUBLANES, lanes), F32)
            for kk in range(TOP_K):
                p = gbuf[g, kk, :, c0:c0 + lanes]
                w = meta[:, META_W + kk:META_W + kk + 1]
                acc_lo = acc_lo + w * lax.bitcast_convert_type(p << 16, F32)
                acc_hi = acc_hi + w * lax.bitcast_convert_type(p & jnp.uint32(_HI16), F32)
            for acc, col in ((acc_lo, c0), (acc_hi, half + c0)):
                cols = slice(col, col + lanes)
                gate = mods_ref[row:row + 1, G2 * d + col:G2 * d + col + lanes]
                out = xb_ref[ro, cols] + gate * acc
                o_ref[ro, cols] = out
                if final:
                    ssq = ssq + jnp.sum(out * out, axis=-1, keepdims=True)
        if final:
            o_ref[ro, :] = o_ref[ro, :] * lax.rsqrt(ssq * (1.0 / d) + EPS) * fn_ref[...]

    n_groups = tc // SUBLANES

    @pl.when(i == 0)
    def _():
        def first(g, carry):
            issue_group(slot_ref, 0, g, gbuf_a, sem_a)
            return carry
        lax.fori_loop(0, n_groups, first, 0)

    wait_all(gbuf_a, sem_a)

    def step_a(g, carry):
        issue_group(slot_ref, tc * TOP_K, g, gbuf_b, sem_b)
        rows(g, gbuf_a, 0)
        return carry
    lax.fori_loop(0, n_groups, step_a, 0)

    wait_all(gbuf_b, sem_b)

    def step_b(g, carry):
        issue_group(next_slot_ref, 0, g, gbuf_a, sem_a)
        rows(g, gbuf_b, tc)
        return carry
    lax.fori_loop(0, n_groups, step_b, 0)

    @pl.when(i == pl.num_programs(0) - 1)
    def _():
        wait_all(gbuf_a, sem_a)


def _combine(ys, slot3, meta, xb, mods_all, layer, row, final_norm=None):
    t, d = xb.shape
    tc = _COMB_TOKENS
    nb = t // (2 * tc)
    assert t % (2 * tc) == 0 and (d // 2) % min(_COMB_LANES, d // 2) == 0
    blk = pl.BlockSpec((2 * tc, d), lambda i: (i, 0))
    in_specs = [pl.BlockSpec((None, 1, 2 * tc * TOP_K), lambda i: (i, 0, 0), memory_space=pltpu.SMEM),
                pl.BlockSpec((None, 1, 2 * tc * TOP_K), lambda i: (jnp.minimum(i + 1, nb - 1), 0, 0),
                             memory_space=pltpu.SMEM),
                pl.BlockSpec((2 * tc, LANES), lambda i: (i, 0)), blk,
                pl.BlockSpec((None, SUBLANES, mods_all.shape[-1]), lambda i: (layer, 0, 0))]
    args = [slot3, slot3, meta, xb, mods_all]
    if final_norm is not None:
        in_specs.append(pl.BlockSpec((1, d), lambda i: (0, 0)))
        args.append(final_norm)
    in_specs.append(pl.BlockSpec(memory_space=pl.ANY))
    args.append(ys)
    return pl.pallas_call(
        functools.partial(_combine_body, row=row, d=d, final=final_norm is not None),
        out_shape=jax.ShapeDtypeStruct((t, d), F32),
        grid=(nb,),
        in_specs=in_specs,
        out_specs=blk,
        scratch_shapes=[pltpu.VMEM((tc // SUBLANES, TOP_K, SUBLANES, d // 2), U32),
                        pltpu.VMEM((tc // SUBLANES, TOP_K, SUBLANES, d // 2), U32),
                        pltpu.SemaphoreType.DMA, pltpu.SemaphoreType.DMA],
        compiler_params=_cparams(("arbitrary",)),
        name="moe_combine",
    )(*args)


def _moe_layer(x, layer, row, mods_all, norm_ffn3, wr_pad, rb_pad, e_w_gate_t, e_w_up_t, e_w_down,
               s_w_gate_t, s_w_up_t, s_w_down, final_norm):
    t, d = x.shape
    n_experts = e_w_down.shape[1]
    u, xb, meta, counts = _router(x, norm_ffn3, layer, mods_all, row, wr_pad, rb_pad,
                                  s_w_gate_t, s_w_up_t, s_w_down, n_experts)
    counts = counts[0, :n_experts].astype(I32)
    ends = jnp.cumsum(counts)
    starts = ends - counts
    idx = meta[:, META_IDX:META_IDX + TOP_K].astype(I32).reshape(-1)
    rank = meta[:, META_RANK:META_RANK + TOP_K].astype(I32).reshape(-1)
    ns = t * TOP_K
    n_tiles = ns // _FFN_ROWS
    first_tile = starts // _FFN_ROWS
    ntile_e = jnp.where(counts > 0, (ends - 1) // _FFN_ROWS - first_tile + 1, 0)
    vend = jnp.cumsum(ntile_e)
    vstart = vend - ntile_e
    nvis = vend[-1]
    nv = n_tiles + n_experts - 1
    v = jnp.minimum(jnp.arange(nv, dtype=I32), nvis - 1)
    ve = jnp.minimum(jnp.sum((vend[None, :] <= v[:, None]).astype(I32), axis=1), n_experts - 1)
    onehot = (ve[:, None] == jnp.arange(n_experts, dtype=I32)[None, :]).astype(I32)
    vt = (jnp.sum(onehot * (first_tile - vstart)[None, :], axis=1) + v).astype(I32)
    starts_ext = jnp.concatenate([starts, ends[-1:]]).astype(I32)

    disp_tokens = min(256, t)
    per = lambda a, n: a.reshape(t // n, 1, n * TOP_K)
    xs, slot3 = _dispatch(u, starts_ext, per(idx, disp_tokens), per(rank, disp_tokens), disp_tokens)
    ys = _grouped_ffn(xs, vt, ve, starts_ext, nvis.reshape(1).astype(I32), e_w_gate_t, e_w_up_t, e_w_down, layer)
    return _combine(ys, per(slot3, 2 * _COMB_TOKENS), meta, xb, mods_all, layer, row, final_norm)


def _pad_cols(a, n):
    return jnp.pad(a, ((0, 0), (0, n - a.shape[1])))


def _mlstm_layer(streams, mods, norm, a_w_in, j, gate_bias, head_norm3, a_w_out, ctx_out):
    H = M_HEADS
    d = a_w_in.shape[1]
    dv = head_norm3.shape[-1] // H
    dk = dv // 2
    n_proj = 2 * H * dk + 2 * H * dv
    a_w_in_t = jnp.swapaxes(a_w_in, 1, 2)
    w_gates = jnp.pad(a_w_in_t[j, n_proj:], ((0, LANES - 4 * H), (0, 0)))
    w_in_t = a_w_in_t.astype(BF16)
    w_out = a_w_out[j].astype(BF16)
    bias_row = _pad_cols(gate_bias.reshape(1, 4 * H), LANES)
    bias_col = bias_row.reshape(LANES, 1)

    projs = []
    for xs, row in streams:
        proj, gates = _fused_matmul(xs, w_in_t, w_t_layer=j, n_out=n_proj, prologue="mod", norm=norm, mods=mods, row=row,
                                    extra_w=w_gates, out_dtype=BF16, pro_rows=64, name="mlstm_in_proj")
        projs.append((proj, gates, gates.T))

    zero = (jnp.zeros((H, dk, dv), F32), jnp.zeros((H, 1, dk), F32), jnp.zeros((H, 1, LANES), F32))
    (px, gx, gxt), (pc, gc, gct) = projs
    hn = (head_norm3, j)
    hcf, st_f = _mlstm_scan(pc, gc, gct, bias_row, bias_col, zero, reverse=False)
    hxf, _ = _mlstm_scan(px, gx, gxt, bias_row, bias_col, st_f, reverse=False)
    yc_pre, st_b = _mlstm_scan(pc, gc, gct, bias_row, bias_col, zero, reverse=True, finalize=(hcf, hn))
    yx_pre, _ = _mlstm_scan(px, gx, gxt, bias_row, bias_col, st_b, reverse=True, finalize=(hxf, hn))

    outs = []
    for (xs, row), y_pre, live in zip(streams, (yx_pre, yc_pre), (True, ctx_out)):
        if live:
            outs.append(_fused_matmul(y_pre, w_out, n_out=d, epilogue="residual", res=xs, mods=mods,
                                      row=row, name="mlstm_out_proj"))
        else:
            outs.append(xs)
    return outs


def _conv_layer(streams, mods, norm, b_w_in, b_w_dw, ln, b_w_out, j, live_flags, seq_lens):
    d = b_w_in.shape[1]
    w_in = b_w_in[j].astype(BF16)
    w_out = b_w_out[j].astype(BF16)
    outs = []
    for (xs, row), live, seq_len in zip(streams, live_flags, seq_lens):
        if not live:
            outs.append(xs)
            continue
        hglu = _fused_matmul(xs, w_in, n_out=d, prologue="mod", norm=norm, mods=mods, row=row,
                             epilogue="glu", w2_col_offset=d, name="conv_in_glu")
        hc = _dwconv(hglu, b_w_dw, j, seq_len=seq_len)
        outs.append(_fused_matmul(hc, w_out, n_out=d, prologue="ln_silu", ln=(ln, j),
                                  epilogue="residual", res=xs, mods=mods, row=row, pro_rows=64,
                                  name="conv_out_proj"))
    return outs


def kernel(x, c, ctx, c_ctx, ada_down, ada_up, ada_bias, norm_mix, norm_ffn, a_w_in, a_gate_bias, a_head_norm, a_w_out, b_w_in, b_w_dw, b_ln_g, b_ln_b, b_w_out, w_router, router_bias, e_w_gate, e_w_up, e_w_down, s_w_gate, s_w_up, s_w_down, final_norm):
    bsz, t, d = x.shape
    assert bsz == 1 and c.shape[0] == 1 and ctx.shape[0] == 1
    depth = ada_down.shape[0]
    n_experts = w_router.shape[-1]
    t_ctx = ctx.shape[1]

    cond = jnp.zeros((SUBLANES, d), F32).at[0].set(c[0]).at[1].set(c_ctx)
    mods_all = _ada_all(cond, ada_down, ada_up, ada_bias)
    norm_mix3 = norm_mix.reshape(depth, 1, d)
    norm_ffn3 = norm_ffn.reshape(depth, 1, d)
    head_norm3 = a_head_norm.reshape(a_head_norm.shape[0], 1, -1)
    ln3 = (b_ln_g.reshape(-1, 1, d), b_ln_b.reshape(-1, 1, d))
    fn = final_norm.reshape(1, d)
    e_gate_t, e_up_t = jnp.swapaxes(e_w_gate, 2, 3), jnp.swapaxes(e_w_up, 2, 3)
    s_gate_t, s_up_t = jnp.swapaxes(s_w_gate, 1, 2), jnp.swapaxes(s_w_up, 1, 2)

    xs, cs = x[0], ctx[0]
    for i in range(depth):
        kind, j = i % 2, i // 2
        ctx_live = any(l % 2 == 0 for l in range(i + 1, depth))
        mods = (mods_all, i)
        norm = (norm_mix3, i)
        streams = [(xs, 0), (cs, 1)]
        if kind == 0:
            xs, cs = _mlstm_layer(streams, mods, norm, a_w_in, j, a_gate_bias[j], head_norm3, a_w_out, ctx_live)
        else:
            xs, cs = _conv_layer(streams, mods, norm, b_w_in, b_w_dw, ln3, b_w_out, j,
                                 (True, ctx_live), (GRID_W, t_ctx))

        wr = jnp.pad(w_router[i].T, ((0, LANES - n_experts), (0, 0)))
        rb = _pad_cols(router_bias[i].reshape(1, n_experts), LANES)
        last = i == depth - 1
        new = []
        for (s, row), live in zip(((xs, 0), (cs, 1)), (True, ctx_live)):
            if live:
                s = _moe_layer(s, i, row, mods_all, norm_ffn3, wr, rb, e_gate_t, e_up_t, e_w_down,
                               s_gate_t, s_up_t, s_w_down, fn if (last and row == 0) else None)
            new.append(s)
        xs, cs = new
    return xs[None]
```

```python
import functools

import jax
import jax.numpy as jnp
from jax import lax
from jax.experimental import pallas as pl
from jax.experimental.pallas import tpu as pltpu

F32 = jnp.float32
BF16 = jnp.bfloat16
I32 = jnp.int32
U32 = jnp.uint32

GRID_W = 64
M_HEADS = 8
M_CHUNK = 128
CONV_W = 31
CONV_HALF = CONV_W // 2
TOP_K = 6
ROUTED_SCALE = 2.5
N_MOD = 6
EPS = 1e-6
SH1, SC1, G1, SH2, SC2, G2 = range(N_MOD)

LANES = 128
SUBLANES = 8
VMEM_LIMIT = 56 * 1024 * 1024
HIGHEST = lax.Precision.HIGHEST


def _cparams(sem):
    return pltpu.CompilerParams(dimension_semantics=sem, vmem_limit_bytes=VMEM_LIMIT)


def _sigmoid(v):
    return 1.0 / (1.0 + jnp.exp(-v))


def _silu(v):
    return v * _sigmoid(v)


def _log_sigmoid(v):
    return jnp.minimum(v, 0.0) - jnp.log(1.0 + jnp.exp(-jnp.abs(v)))


def _dot_nt(a, b_t):
    return lax.dot_general(a, b_t, (((1,), (1,)), ((), ())), preferred_element_type=F32)


_HI16 = 0xFFFF0000


def _pack_bf16_pairs(u):
    half = u.shape[1] // 2
    bits = lax.bitcast_convert_type(u.astype(F32), U32)
    return (bits[:, half:] & jnp.uint32(_HI16)) | (bits[:, :half] >> 16)


def _unpack_bf16_pairs(p):
    lo = lax.bitcast_convert_type(p << 16, F32)
    hi = lax.bitcast_convert_type(p & jnp.uint32(_HI16), F32)
    return jnp.concatenate([lo, hi], axis=1).astype(BF16)


def _rmsnorm(x, g):
    return x * lax.rsqrt(jnp.mean(x * x, axis=-1, keepdims=True) + EPS) * g


def _ada_body(cond_ref, down_ref, up_ref, bias_ref, o_ref):
    s = _silu(cond_ref[...])
    t = jnp.dot(s, down_ref[...], precision=HIGHEST, preferred_element_type=F32)
    o_ref[...] = jnp.dot(t, up_ref[...], precision=HIGHEST, preferred_element_type=F32) + bias_ref[...]


def _ada_all(cond, ada_down, ada_up, ada_bias):
    n_layers, d, rank = ada_down.shape
    n_out = ada_up.shape[-1]
    tn = d
    assert n_out % tn == 0
    return pl.pallas_call(
        _ada_body,
        out_shape=jax.ShapeDtypeStruct((n_layers, SUBLANES, n_out), F32),
        grid=(n_layers, n_out // tn),
        in_specs=[
            pl.BlockSpec((SUBLANES, d), lambda l, j: (0, 0)),
            pl.BlockSpec((None, d, rank), lambda l, j: (l, 0, 0)),
            pl.BlockSpec((None, rank, tn), lambda l, j: (l, 0, j)),
            pl.BlockSpec((None, 1, tn), lambda l, j: (l, 0, j)),
        ],
        out_specs=pl.BlockSpec((None, SUBLANES, tn), lambda l, j: (l, 0, j)),
        compiler_params=_cparams(("parallel", "arbitrary")),
        name="ada_mod",
    )(cond, ada_down, ada_up, ada_bias.reshape(n_layers, 1, n_out))


_PRO_ROWS = 32


def _mm_body(*refs, prologue, epilogue, extra, tm, row, d_mod, w_transposed, pro_rows):
    refs = list(refs)
    a_ref = refs.pop(0)
    if prologue == "mod":
        g_ref, mods_ref = refs.pop(0), refs.pop(0)
    elif prologue == "ln_silu":
        g_ref, b_ref = refs.pop(0), refs.pop(0)
    w_ref = refs.pop(0)
    if epilogue == "glu":
        w2_ref = refs.pop(0)
    elif epilogue == "residual":
        res_ref, gate_ref = refs.pop(0), refs.pop(0)
    if extra:
        wx_ref = refs.pop(0)
    o_ref = refs.pop(0)
    if extra:
        ox_ref = refs.pop(0)

    if prologue == "none":
        u = a_ref[...]
    else:
        u_ref = refs.pop(0)

        @pl.when(pl.program_id(1) == 0)
        def _():
            def rows(r, carry):
                sl = pl.ds(pl.multiple_of(r * pro_rows, pro_rows), pro_rows)
                x = a_ref[sl, :]
                if prologue == "mod":
                    shift = mods_ref[row:row + 1, SH1 * d_mod:(SH1 + 1) * d_mod]
                    scale = mods_ref[row:row + 1, SC1 * d_mod:(SC1 + 1) * d_mod]
                    y = _rmsnorm(x, g_ref[...]) * (1.0 + scale) + shift
                else:
                    mu = jnp.mean(x, axis=-1, keepdims=True)
                    xc = x - mu
                    var = jnp.mean(xc * xc, axis=-1, keepdims=True)
                    y = _silu(xc * lax.rsqrt(var + EPS) * g_ref[...] + b_ref[...])
                u_ref[sl, :] = y.astype(BF16)
                return carry

            lax.fori_loop(0, tm // pro_rows, rows, 0)

        u = u_ref[...]

    if extra:
        @pl.when(pl.program_id(1) == 0)
        def _():
            ox_ref[...] = _dot_nt(u, wx_ref[...].astype(BF16))

    if w_transposed:
        acc = _dot_nt(u, w_ref[...])
    else:
        acc = jnp.dot(u, w_ref[...], preferred_element_type=F32)
    if epilogue == "glu":
        acc2 = jnp.dot(u, w2_ref[...], preferred_element_type=F32)
        acc = acc * _sigmoid(acc2)
    elif epilogue == "residual":
        acc = res_ref[...] + gate_ref[row:row + 1, :] * acc
    o_ref[...] = acc.astype(o_ref.dtype)


def _fused_matmul(a, w, *, n_out, prologue="none", norm=None, ln=None, mods=None, row=0,
                  epilogue="plain", res=None, w2_col_offset=0, extra_w=None, w_t_layer=None,
                  out_dtype=F32, tm=1024, tn=512, pro_rows=_PRO_ROWS, name="mm"):
    m, k = a.shape
    tm = min(tm, m)
    tn = min(tn, n_out)
    assert m % tm == 0 and n_out % tn == 0 and w.dtype == BF16 and tm % pro_rows == 0
    assert w.shape[-1 if w_t_layer is not None else 0] == k
    d_mod = None
    a_mode = {} if prologue == "none" else dict(pipeline_mode=pl.Buffered(1))
    in_specs = [pl.BlockSpec((tm, k), lambda i, j: (i, 0), **a_mode)]
    args = [a]
    if prologue == "mod":
        norm_arr, norm_layer = norm
        mods_all, mods_layer = mods
        d_mod = mods_all.shape[-1] // N_MOD
        in_specs += [pl.BlockSpec((None, 1, k), lambda i, j: (norm_layer, 0, 0)),
                     pl.BlockSpec((None, SUBLANES, mods_all.shape[-1]), lambda i, j: (mods_layer, 0, 0))]
        args += [norm_arr, mods_all]
    elif prologue == "ln_silu":
        (ln_g, ln_b), ln_layer = ln
        in_specs += [pl.BlockSpec((None, 1, k), lambda i, j: (ln_layer, 0, 0))] * 2
        args += [ln_g, ln_b]
    if w_t_layer is not None:
        in_specs.append(pl.BlockSpec((None, tn, k), lambda i, j: (w_t_layer, j, 0)))
    else:
        in_specs.append(pl.BlockSpec((k, tn), lambda i, j: (0, j)))
    args.append(w)
    if epilogue == "glu":
        off = w2_col_offset // tn
        in_specs.append(pl.BlockSpec((k, tn), lambda i, j: (0, j + off)))
        args.append(w)
    elif epilogue == "residual":
        mods_all, mods_layer = mods
        goff = G1 * (mods_all.shape[-1] // N_MOD) // tn
        in_specs += [pl.BlockSpec((tm, tn), lambda i, j: (i, j)),
                     pl.BlockSpec((None, SUBLANES, tn), lambda i, j: (mods_layer, 0, goff + j))]
        args += [res, mods_all]
    out_shape = [jax.ShapeDtypeStruct((m, n_out), out_dtype)]
    out_specs = [pl.BlockSpec((tm, tn), lambda i, j: (i, j))]
    if extra_w is not None:
        in_specs.append(pl.BlockSpec((LANES, k), lambda i, j: (0, 0)))
        args.append(extra_w)
        out_shape.append(jax.ShapeDtypeStruct((m, LANES), F32))
        out_specs.append(pl.BlockSpec((tm, LANES), lambda i, j: (i, 0)))
    scratch = [] if prologue == "none" else [pltpu.VMEM((tm, k), BF16)]
    outs = pl.pallas_call(
        functools.partial(_mm_body, prologue=prologue, epilogue=epilogue, extra=extra_w is not None,
                          w_transposed=w_t_layer is not None,
                          tm=tm, row=row, d_mod=d_mod, pro_rows=pro_rows),
        out_shape=out_shape,
        grid=(m // tm, n_out // tn),
        in_specs=in_specs,
        out_specs=out_specs,
        scratch_shapes=scratch,
        compiler_params=_cparams(("parallel", "arbitrary")),
        name=name,
    )(*args)
    return outs if extra_w is not None else outs[0]


def _scan_body(*refs, reverse, finalize, dk, dv):
    refs = list(refs)
    q_ref, k_ref, v_ref = refs.pop(0), refs.pop(0), refs.pop(0)
    gcol_ref, grow_ref, bias_row_ref, bias_col_ref = refs.pop(0), refs.pop(0), refs.pop(0), refs.pop(0)
    c0_ref, n0_ref, m0_ref = refs.pop(0), refs.pop(0), refs.pop(0)
    if finalize:
        o_ref, hprev_ref, hnorm_ref = refs.pop(0), refs.pop(0), refs.pop(0)
    out_ref, c_ref, n_ref, m_ref = refs.pop(0), refs.pop(0), refs.pop(0), refs.pop(0)

    L = M_CHUNK
    H = M_HEADS

    @pl.when(pl.program_id(0) == 0)
    def _():
        c_ref[...] = c0_ref[...]
        n_ref[...] = n0_ref[...]
        m_ref[...] = m0_ref[...]

    kind_i = 2 if reverse else 0
    kind_f = kind_i + 1

    t_idx = lax.broadcasted_iota(I32, (L, L), 0)
    s_idx = lax.broadcasted_iota(I32, (L, L), 1)
    mask = (s_idx >= t_idx) if reverse else (s_idx <= t_idx)
    mask_f = mask.astype(F32)
    mask_t = ((t_idx >= s_idx) if reverse else (t_idx <= s_idx)).astype(F32)
    q_scale = dk ** -0.5

    gcol = gcol_ref[...] + bias_row_ref[...]
    grow = grow_ref[...] + bias_col_ref[...]
    lf_col = _log_sigmoid(gcol)
    lf_row = _log_sigmoid(grow[kind_f * H:(kind_f + 1) * H, :])
    b_col_all = jnp.dot(mask_f, lf_col, precision=HIGHEST, preferred_element_type=F32)
    b_row_all = jnp.dot(lf_row, mask_t, precision=HIGHEST, preferred_element_type=F32)

    qs = lambda h: q_ref[:, h * dk:(h + 1) * dk]
    ks = lambda h: k_ref[:, h * dk:(h + 1) * dk]
    vs = lambda h: v_ref[:, h * dv:(h + 1) * dv]

    s_raw = [_dot_nt(qs(h), ks(h)) for h in range(H)]
    qc = [jnp.dot(qs(h), c_ref[h].astype(BF16), preferred_element_type=F32) for h in range(H)]

    st = []
    for h in range(H):
        ci, cf = kind_i * H + h, kind_f * H + h
        b_col = b_col_all[:, cf:cf + 1]
        b_row = b_row_all[h:h + 1, :]
        i_col = gcol[:, ci:ci + 1]
        i_row = grow[ci:ci + 1, :]
        m_prev = m_ref[h][:, :1]
        dmat = jnp.where(mask, b_col - b_row + i_row, -jnp.inf)
        m_inter = b_col + m_prev
        m_t = jnp.maximum(m_inter, jnp.max(dmat, axis=-1, keepdims=True))
        w_intra = jnp.exp(dmat - m_t) * q_scale
        w_inter = jnp.exp(m_inter - m_t) * q_scale
        b_last = jnp.sum(lf_row[h:h + 1, :], axis=-1, keepdims=True)
        m_new = jnp.maximum(b_last + m_prev, jnp.max(b_last - b_row + i_row, axis=-1, keepdims=True))
        kw = ks(h).astype(F32) * jnp.exp(b_last - b_col + i_col - m_new)
        decay = jnp.exp(b_last + m_prev - m_new)
        s = s_raw[h] * w_intra
        st.append(dict(m_t=m_t, w_inter=w_inter, m_new=m_new, kw=kw, decay=decay, s=s))

    sv = [jnp.dot(st[h]["s"].astype(BF16), vs(h), preferred_element_type=F32) for h in range(H)]
    kv = [lax.dot_general(st[h]["kw"].astype(BF16), vs(h), (((0,), (0,)), ((), ())),
                          preferred_element_type=F32) for h in range(H)]

    for h in range(H):
        a = st[h]
        n_state = n_ref[h]
        num = sv[h] + a["w_inter"] * qc[h]
        qn = jnp.sum(qs(h).astype(F32) * n_state, axis=-1, keepdims=True)
        den = jnp.sum(a["s"], axis=-1, keepdims=True) + a["w_inter"] * qn
        hv = num * (1.0 / jnp.maximum(jnp.abs(den), jnp.exp(-a["m_t"])))

        c_ref[h] = a["decay"] * c_ref[h] + kv[h]
        n_ref[h] = a["decay"] * n_state + jnp.sum(a["kw"], axis=0, keepdims=True)
        m_ref[h] = jnp.broadcast_to(a["m_new"], (1, LANES))

        cols = slice(h * dv, (h + 1) * dv)
        if finalize:
            ht = hv + hprev_ref[:, cols]
            ht = ht * lax.rsqrt(jnp.mean(ht * ht, axis=-1, keepdims=True) + EPS)
            ht = ht * hnorm_ref[:, cols] * _sigmoid(o_ref[:, cols].astype(F32))
            out_ref[:, cols] = ht.astype(out_ref.dtype)
        else:
            out_ref[:, cols] = hv


def _mlstm_scan(proj, gcol, grow, bias_row, bias_col, state, *, reverse, finalize=None):
    t = proj.shape[0]
    H, L = M_HEADS, M_CHUNK
    c0, n0, m0 = state
    dk, dv = c0.shape[1], c0.shape[2]
    nc = t // L
    cidx = (lambda i: nc - 1 - i) if reverse else (lambda i: i)
    qk_w, v_w = H * dk, H * dv
    assert 2 * qk_w == v_w
    full = lambda shape: pl.BlockSpec(shape, lambda i: (0,) * len(shape))
    in_specs = [
        pl.BlockSpec((L, qk_w), lambda i: (cidx(i), 0)),
        pl.BlockSpec((L, qk_w), lambda i: (cidx(i), 1)),
        pl.BlockSpec((L, v_w), lambda i: (cidx(i), 1)),
        pl.BlockSpec((L, LANES), lambda i: (cidx(i), 0)),
        pl.BlockSpec((LANES, L), lambda i: (0, cidx(i))),
        full((1, LANES)), full((LANES, 1)),
        full(c0.shape), full(n0.shape), full(m0.shape),
    ]
    args = [proj, proj, proj, gcol, grow, bias_row, bias_col, c0, n0, m0]
    if finalize is not None:
        hprev, (hnorm, hn_layer) = finalize
        in_specs += [pl.BlockSpec((L, v_w), lambda i: (cidx(i), 2)),
                     pl.BlockSpec((L, v_w), lambda i: (cidx(i), 0)),
                     pl.BlockSpec((None, 1, v_w), lambda i: (hn_layer, 0, 0))]
        args += [proj, hprev, hnorm]
    out_dtype = BF16 if finalize is not None else F32
    out, c1, n1, m1 = pl.pallas_call(
        functools.partial(_scan_body, reverse=reverse, finalize=finalize is not None, dk=dk, dv=dv),
        out_shape=(jax.ShapeDtypeStruct((t, v_w), out_dtype),
                   jax.ShapeDtypeStruct(c0.shape, F32),
                   jax.ShapeDtypeStruct(n0.shape, F32),
                   jax.ShapeDtypeStruct(m0.shape, F32)),
        grid=(nc,),
        in_specs=in_specs,
        out_specs=(pl.BlockSpec((L, v_w), lambda i: (cidx(i), 0)),
                   full(c0.shape), full(n0.shape), full(m0.shape)),
        compiler_params=_cparams(("arbitrary",)),
        name="mlstm_scan_bwd" if reverse else "mlstm_scan_fwd",
    )(*args)
    return out, (c1, n1, m1)


_CONV_LANES = 512
_CONV_ROWS = 64
_CONV_PAD = 16


def _conv_body(x_ref, w_ref, o_ref, pad_ref, shift_ref, *, seq_len, n_seq):
    zeros = jnp.zeros((_CONV_PAD, _CONV_LANES), F32)
    pad_ref[0:_CONV_PAD, :] = zeros
    pad_ref[_CONV_PAD + seq_len:2 * _CONV_PAD + seq_len, :] = zeros
    n_shift_rows = shift_ref.shape[1]

    def one_seq(s, carry):
        base = pl.multiple_of(s * seq_len, seq_len)
        pad_ref[_CONV_PAD:_CONV_PAD + seq_len, :] = x_ref[pl.ds(base, seq_len), :]
        for b in range(1, SUBLANES):
            shift_ref[b] = pad_ref[b:b + n_shift_rows, :]
        for r0 in range(0, seq_len, _CONV_ROWS):
            acc = jnp.zeros((_CONV_ROWS, _CONV_LANES), F32)
            for j in range(CONV_W):
                start = _CONV_PAD - CONV_HALF + j + r0
                b, a = start % SUBLANES, start - start % SUBLANES
                src = pad_ref if b == 0 else shift_ref.at[b]
                acc = acc + w_ref[j:j + 1, :] * src[a:a + _CONV_ROWS, :]
            o_ref[pl.ds(base + r0, _CONV_ROWS), :] = acc
        return carry

    lax.fori_loop(0, n_seq, one_seq, 0)


def _dwconv(x, w_dw, layer, *, seq_len):
    t, d = x.shape
    n_seq = max(1, min(512, t) // seq_len)
    rows = n_seq * seq_len
    assert t % rows == 0 and d % _CONV_LANES == 0 and seq_len % _CONV_ROWS == 0
    return pl.pallas_call(
        functools.partial(_conv_body, seq_len=seq_len, n_seq=n_seq),
        out_shape=jax.ShapeDtypeStruct((t, d), F32),
        grid=(t // rows, d // _CONV_LANES),
        in_specs=[pl.BlockSpec((rows, _CONV_LANES), lambda i, j: (i, j)),
                  pl.BlockSpec((None, CONV_W, _CONV_LANES), lambda i, j: (layer, 0, j))],
        out_specs=pl.BlockSpec((rows, _CONV_LANES), lambda i, j: (i, j)),
        scratch_shapes=[pltpu.VMEM((seq_len + 2 * _CONV_PAD, _CONV_LANES), F32),
                        pltpu.VMEM((SUBLANES, seq_len + 2 * _CONV_PAD - SUBLANES, _CONV_LANES), F32)],
        compiler_params=_cparams(("parallel", "parallel")),
        name="dwconv",
    )(x, w_dw)


META_IDX, META_RANK, META_W = 0, 8, 16


def _router_body(x_ref, g_ref, mods_ref, wr_ref, rb_ref, swg_ref, swu_ref, swd_ref,
                 u_ref, xb_ref, meta_ref, counts_ref, carry_ref, *, n_experts, row, d, tm):
    @pl.when(pl.program_id(0) == 0)
    def _():
        carry_ref[...] = jnp.zeros_like(carry_ref)

    x = x_ref[...]
    shift = mods_ref[row:row + 1, SH2 * d:(SH2 + 1) * d]
    scale = mods_ref[row:row + 1, SC2 * d:(SC2 + 1) * d]
    gate2 = mods_ref[row:row + 1, G2 * d:(G2 + 1) * d]
    uf = _rmsnorm(x, g_ref[...]) * (1.0 + scale) + shift
    u = uf.astype(BF16)
    u_ref[...] = _pack_bf16_pairs(u)

    hs = _silu(_dot_nt(u, swg_ref[...].astype(BF16)))
    hs = hs * _dot_nt(u, swu_ref[...].astype(BF16))
    shared = jnp.dot(hs.astype(BF16), swd_ref[...].astype(BF16), preferred_element_type=F32)
    xb_ref[...] = x + gate2 * shared

    logits = _dot_nt(u, wr_ref[...].astype(BF16))
    scores = _sigmoid(logits)
    lane = lax.broadcasted_iota(I32, scores.shape, 1)
    sel = jnp.where(lane < n_experts, scores + rb_ref[...], -jnp.inf)
    picked = jnp.zeros_like(scores)
    chosen = jnp.zeros_like(scores)
    firsts = []
    for _ in range(TOP_K):
        mx = jnp.max(sel, axis=-1, keepdims=True)
        first = jnp.min(jnp.where(sel == mx, lane, LANES), axis=-1, keepdims=True)
        hit = lane == first
        picked = jnp.where(hit, scores, picked)
        chosen = jnp.where(hit, 1.0, chosen)
        sel = jnp.where(hit, -jnp.inf, sel)
        firsts.append(first)
    gates = picked / jnp.sum(picked, axis=-1, keepdims=True) * ROUTED_SCALE

    r_idx = lax.broadcasted_iota(I32, (tm, tm), 0)
    c_idx = lax.broadcasted_iota(I32, (tm, tm), 1)
    before = (c_idx < r_idx).astype(BF16)
    pos = jnp.dot(before, chosen.astype(BF16), preferred_element_type=F32) + carry_ref[...]
    meta = jnp.zeros_like(scores)
    for kk, first in enumerate(firsts):
        hit = lane == first
        rank = jnp.sum(jnp.where(hit, pos, 0.0), axis=-1, keepdims=True)
        wk = jnp.sum(jnp.where(hit, gates, 0.0), axis=-1, keepdims=True)
        meta = jnp.where(lane == META_IDX + kk, first.astype(F32), meta)
        meta = jnp.where(lane == META_RANK + kk, rank, meta)
        meta = jnp.where(lane == META_W + kk, wk, meta)
    meta_ref[...] = meta
    carry_ref[...] += jnp.sum(chosen, axis=0, keepdims=True)
    counts_ref[...] = carry_ref[...]


def _router(x, norm_ffn3, layer, mods_all, row, wr_pad, rb_pad, s_w_gate_t, s_w_up_t, s_w_down, n_experts, tm=256):
    t, d = x.shape
    tm = min(tm, t)
    ds_ = s_w_gate_t.shape[1]
    c2 = lambda i: (0, 0)
    lay3 = lambda i: (layer, 0, 0)
    blk = pl.BlockSpec((tm, d), lambda i: (i, 0))
    return pl.pallas_call(
        functools.partial(_router_body, n_experts=n_experts, row=row, d=d, tm=tm),
        out_shape=(jax.ShapeDtypeStruct((t, d // 2), U32), jax.ShapeDtypeStruct((t, d), F32),
                   jax.ShapeDtypeStruct((t, LANES), F32), jax.ShapeDtypeStruct((1, LANES), F32)),
        grid=(t // tm,),
        in_specs=[blk,
                  pl.BlockSpec((None, 1, d), lay3),
                  pl.BlockSpec((None, SUBLANES, mods_all.shape[-1]), lay3),
                  pl.BlockSpec((LANES, d), c2), pl.BlockSpec((1, LANES), c2),
                  pl.BlockSpec((None, ds_, d), lay3), pl.BlockSpec((None, ds_, d), lay3),
                  pl.BlockSpec((None, ds_, d), lay3)],
        out_specs=(pl.BlockSpec((tm, d // 2), lambda i: (i, 0)), blk,
                   pl.BlockSpec((tm, LANES), lambda i: (i, 0)), pl.BlockSpec((1, LANES), c2)),
        scratch_shapes=[pltpu.VMEM((1, LANES), F32)],
        compiler_params=_cparams(("arbitrary",)),
        name="moe_router",
    )(x, norm_ffn3, mods_all, wr_pad, rb_pad, s_w_gate_t, s_w_up_t, s_w_down)


def _dispatch_body(starts_ref, idx_ref, rank_ref, u_ref, xs_hbm, slot_ref, sem, *, tokens):
    def issue(g, carry):
        base = pl.multiple_of(g * SUBLANES, SUBLANES)
        for j in range(SUBLANES):
            for kk in range(TOP_K):
                a = (base + j) * TOP_K + kk
                s = starts_ref[idx_ref[0, a]] + rank_ref[0, a]
                slot_ref[0, a] = s
                pltpu.make_async_copy(u_ref.at[g, pl.ds(j, 1), :], xs_hbm.at[pl.ds(s, 1), :],
                                      sem).start(priority=kk % 2)
        return carry

    lax.fori_loop(0, tokens // SUBLANES, issue, 0)
    for _ in range(TOP_K):
        pltpu.make_async_copy(xs_hbm.at[pl.ds(0, tokens), :], xs_hbm.at[pl.ds(0, tokens), :], sem).wait()


def _dispatch(u, starts_ext, idx3, rank3, tokens):
    t, d = u.shape
    smem_blk = pl.BlockSpec((None, 1, tokens * TOP_K), lambda i, st: (i, 0, 0), memory_space=pltpu.SMEM)
    u = u.reshape(t // SUBLANES, SUBLANES, d)
    grid_spec = pltpu.PrefetchScalarGridSpec(
        num_scalar_prefetch=1,
        grid=(t // tokens,),
        in_specs=[smem_blk, smem_blk, pl.BlockSpec((tokens // SUBLANES, SUBLANES, d), lambda i, st: (i, 0, 0))],
        out_specs=(pl.BlockSpec(memory_space=pl.ANY), smem_blk),
        scratch_shapes=[pltpu.SemaphoreType.DMA],
    )
    return pl.pallas_call(
        functools.partial(_dispatch_body, tokens=tokens),
        out_shape=(jax.ShapeDtypeStruct((t * TOP_K, d), u.dtype), jax.ShapeDtypeStruct(idx3.shape, I32)),
        grid_spec=grid_spec,
        compiler_params=_cparams(("arbitrary",)),
        name="moe_dispatch",
    )(starts_ext, idx3, rank3, u)


_FFN_ROWS = 256


def _ffn_body(vt_ref, ve_ref, starts_ref, nvis_ref, xs_ref, wg_ref, wu_ref, wd_ref, ys_ref):
    v = pl.program_id(0)

    @pl.when(v < nvis_ref[0])
    def _():
        tile = vt_ref[v]
        e = ve_ref[v]
        x = _unpack_bf16_pairs(xs_ref[...])
        hg = _dot_nt(x, wg_ref[...].astype(BF16))
        hu = _dot_nt(x, wu_ref[...].astype(BF16))
        hh = (_silu(hg) * hu).astype(BF16)
        y = jnp.dot(hh, wd_ref[...].astype(BF16), preferred_element_type=F32)
        y = _pack_bf16_pairs(y.astype(BF16))
        first = jnp.logical_or(v == 0, vt_ref[jnp.maximum(v - 1, 0)] != tile)

        @pl.when(first)
        def _():
            ys_ref[...] = y

        @pl.when(jnp.logical_not(first))
        def _():
            r = lax.broadcasted_iota(I32, (_FFN_ROWS, 1), 0) + tile * _FFN_ROWS
            mine = jnp.logical_and(r >= starts_ref[e], r < starts_ref[e + 1])
            ys_ref[...] = jnp.where(mine, y, ys_ref[...])


def _grouped_ffn(xs, vt, ve, starts, nvis, e_w_gate_t, e_w_up_t, e_w_down, layer):
    ns = xs.shape[0]
    di, d = e_w_down.shape[2:]
    nv = vt.shape[0]
    w_blk = pl.BlockSpec((None, None, di, d), lambda v, vt, ve, st, nn: (layer, ve[v], 0, 0))
    grid_spec = pltpu.PrefetchScalarGridSpec(
        num_scalar_prefetch=4,
        grid=(nv,),
        in_specs=[pl.BlockSpec((_FFN_ROWS, d // 2), lambda v, vt, ve, st, nn: (vt[v], 0)), w_blk, w_blk, w_blk],
        out_specs=pl.BlockSpec((_FFN_ROWS, d // 2), lambda v, vt, ve, st, nn: (vt[v], 0)),
    )
    return pl.pallas_call(
        _ffn_body,
        out_shape=jax.ShapeDtypeStruct((ns, d // 2), U32),
        grid_spec=grid_spec,
        compiler_params=_cparams(("arbitrary",)),
        name="moe_grouped_ffn",
    )(vt, ve, starts, nvis, xs, e_w_gate_t, e_w_up_t, e_w_down)


_COMB_TOKENS = 128


_COMB_LANES = 1024


def _combine_body(*refs, row, d, final):
    refs = list(refs)
    slot_ref, next_slot_ref = refs.pop(0), refs.pop(0)
    meta_ref, xb_ref, mods_ref = refs.pop(0), refs.pop(0), refs.pop(0)
    fn_ref = refs.pop(0) if final else None
    ys_hbm, o_ref, gbuf_a, gbuf_b, sem_a, sem_b = refs
    half = d // 2
    lanes = min(_COMB_LANES, half)
    tc = _COMB_TOKENS
    i = pl.program_id(0)

    def issue_group(slots, slot_off, g, gbuf, sem):
        base = pl.multiple_of(g * SUBLANES, SUBLANES)
        for j in range(SUBLANES):
            for kk in range(TOP_K):
                s = slots[0, slot_off + (base + j) * TOP_K + kk]
                pltpu.make_async_copy(ys_hbm.at[pl.ds(s, 1), :], gbuf.at[g, kk, pl.ds(j, 1), :],
                                      sem).start(priority=kk % 2)

    def wait_all(gbuf, sem):
        for kk in range(TOP_K):
            pltpu.make_async_copy(ys_hbm.at[pl.ds(0, tc), :], ys_hbm.at[pl.ds(0, tc), :], sem).wait()

    def rows(g, gbuf, row_off):
        ro =pl.ds(pl.multiple_of(row_off + g * SUBLANES, SUBLANES), SUBLANES)
        meta = meta_ref[ro, :]
        ssq = jnp.zeros((SUBLANES, 1), F32)
        for c0 in range(0, half, lanes):
            acc_lo = jnp.zeros((SUBLANES, lanes), F32)
            acc_hi = jnp.zeros((SUBLANES, lanes), F32)
            for kk in range(TOP_K):
                p = gbuf[g, kk, :, c0:c0 + lanes]
                w = meta[:, META_W + kk:META_W + kk + 1]
                acc_lo = acc_lo + w * lax.bitcast_convert_type(p << 16, F32)
                acc_hi = acc_hi + w * lax.bitcast_convert_type(p & jnp.uint32(_HI16), F32)
            for acc, col in ((acc_lo, c0), (acc_hi, half + c0)):
                cols = slice(col, col + lanes)
                gate = mods_ref[row:row + 1, G2 * d + col:G2 * d + col + lanes]
                out = xb_ref[ro, cols] + gate * acc
                o_ref[ro, cols] = out
                if final:
                    ssq = ssq + jnp.sum(out * out, axis=-1, keepdims=True)
        if final:
            o_ref[ro, :] = o_ref[ro, :] * lax.rsqrt(ssq * (1.0 / d) + EPS) * fn_ref[...]

    n_groups = tc // SUBLANES

    @pl.when(i == 0)
    def _():
        def first(g, carry):
            issue_group(slot_ref, 0, g, gbuf_a, sem_a)
            return carry
        lax.fori_loop(0, n_groups, first, 0)

    wait_all(gbuf_a, sem_a)

    def step_a(g, carry):
        issue_group(slot_ref, tc * TOP_K, g, gbuf_b, sem_b)
        rows(g, gbuf_a, 0)
        return carry
    lax.fori_loop(0, n_groups, step_a, 0)

    wait_all(gbuf_b, sem_b)

    def step_b(g, carry):
        issue_group(next_slot_ref, 0, g, gbuf_a, sem_a)
        rows(g, gbuf_b, tc)
        return carry
    lax.fori_loop(0, n_groups, step_b, 0)

    @pl.when(i == pl.num_programs(0) - 1)
    def _():
        wait_all(gbuf_a, sem_a)


def _combine(ys, slot3, meta, xb, mods_all, layer, row, final_norm=None):
    t, d = xb.shape
    tc = _COMB_TOKENS
    nb = t // (2 * tc)
    assert t % (2 * tc) == 0 and (d // 2) % min(_COMB_LANES, d // 2) == 0
    blk = pl.BlockSpec((2 * tc, d), lambda i: (i, 0))
    in_specs = [pl.BlockSpec((None, 1, 2 * tc * TOP_K), lambda i: (i, 0, 0), memory_space=pltpu.SMEM),
                pl.BlockSpec((None, 1, 2 * tc * TOP_K), lambda i: (jnp.minimum(i + 1, nb - 1), 0, 0),
                             memory_space=pltpu.SMEM),
                pl.BlockSpec((2 * tc, LANES), lambda i: (i, 0)), blk,
                pl.BlockSpec((None, SUBLANES, mods_all.shape[-1]), lambda i: (layer, 0, 0))]
    args = [slot3, slot3, meta, xb, mods_all]
    if final_norm is not None:
        in_specs.append(pl.BlockSpec((1, d), lambda i: (0, 0)))
        args.append(final_norm)
    in_specs.append(pl.BlockSpec(memory_space=pl.ANY))
    args.append(ys)
    return pl.pallas_call(
        functools.partial(_combine_body, row=row, d=d, final=final_norm is not None),
        out_shape=jax.ShapeDtypeStruct((t, d), F32),
        grid=(nb,),
        in_specs=in_specs,
        out_specs=blk,
        scratch_shapes=[pltpu.VMEM((tc // SUBLANES, TOP_K, SUBLANES, d // 2), U32),
                        pltpu.VMEM((tc // SUBLANES, TOP_K, SUBLANES, d // 2), U32),
                        pltpu.SemaphoreType.DMA, pltpu.SemaphoreType.DMA],
        compiler_params=_cparams(("arbitrary",)),
        name="moe_combine",
    )(*args)


def _moe_layer(x, layer, row, mods_all, norm_ffn3, wr_pad, rb_pad, e_w_gate_t, e_w_up_t, e_w_down,
               s_w_gate_t, s_w_up_t, s_w_down, final_norm):
    t, d = x.shape
    n_experts = e_w_down.shape[1]
    u, xb, meta, counts = _router(x, norm_ffn3, layer, mods_all, row, wr_pad, rb_pad,
                                  s_w_gate_t, s_w_up_t, s_w_down, n_experts)
    counts = counts[0, :n_experts].astype(I32)
    ends = jnp.cumsum(counts)
    starts = ends - counts
    idx = meta[:, META_IDX:META_IDX + TOP_K].astype(I32).reshape(-1)
    rank = meta[:, META_RANK:META_RANK + TOP_K].astype(I32).reshape(-1)
    ns = t * TOP_K
    n_tiles = ns // _FFN_ROWS
    first_tile = starts // _FFN_ROWS
    ntile_e = jnp.where(counts > 0, (ends - 1) // _FFN_ROWS - first_tile + 1, 0)
    vend = jnp.cumsum(ntile_e)
    vstart = vend - ntile_e
    nvis = vend[-1]
    nv = n_tiles + n_experts - 1
    v = jnp.minimum(jnp.arange(nv, dtype=I32), nvis - 1)
    ve = jnp.minimum(jnp.sum((vend[None, :] <= v[:, None]).astype(I32), axis=1), n_experts - 1)
    onehot = (ve[:, None] == jnp.arange(n_experts, dtype=I32)[None, :]).astype(I32)
    vt = (jnp.sum(onehot * (first_tile - vstart)[None, :], axis=1) + v).astype(I32)
    starts_ext = jnp.concatenate([starts, ends[-1:]]).astype(I32)

    disp_tokens = min(256, t)
    per = lambda a, n: a.reshape(t // n, 1, n * TOP_K)
    xs, slot3 = _dispatch(u, starts_ext, per(idx, disp_tokens), per(rank, disp_tokens), disp_tokens)
    ys = _grouped_ffn(xs, vt, ve, starts_ext, nvis.reshape(1).astype(I32), e_w_gate_t, e_w_up_t, e_w_down, layer)
    return _combine(ys, per(slot3, 2 * _COMB_TOKENS), meta, xb, mods_all, layer, row, final_norm)


def _pad_cols(a, n):
    return jnp.pad(a, ((0, 0), (0, n - a.shape[1])))


def _mlstm_layer(streams, mods, norm, a_w_in, j, gate_bias, head_norm3, a_w_out, ctx_out):
    H = M_HEADS
    d = a_w_in.shape[1]
    dv = head_norm3.shape[-1] // H
    dk = dv // 2
    n_proj = 2 * H * dk + 2 * H * dv
    a_w_in_t = jnp.swapaxes(a_w_in, 1, 2)
    w_gates = jnp.pad(a_w_in_t[j, n_proj:], ((0, LANES - 4 * H), (0, 0)))
    w_in_t = a_w_in_t.astype(BF16)
    w_out = a_w_out[j].astype(BF16)
    bias_row = _pad_cols(gate_bias.reshape(1, 4 * H), LANES)
    bias_col = bias_row.reshape(LANES, 1)

    projs = []
    for xs, row in streams:
        proj, gates = _fused_matmul(xs, w_in_t, w_t_layer=j, n_out=n_proj, prologue="mod", norm=norm, mods=mods, row=row,
                                    extra_w=w_gates, out_dtype=BF16, pro_rows=64, name="mlstm_in_proj")
        projs.append((proj, gates, gates.T))

    zero = (jnp.zeros((H, dk, dv), F32), jnp.zeros((H, 1, dk), F32), jnp.zeros((H, 1, LANES), F32))
    (px, gx, gxt), (pc, gc, gct) = projs
    hn = (head_norm3, j)
    hcf, st_f = _mlstm_scan(pc, gc, gct, bias_row, bias_col, zero, reverse=False)
    hxf, _ = _mlstm_scan(px, gx, gxt, bias_row, bias_col, st_f, reverse=False)
    yc_pre, st_b = _mlstm_scan(pc, gc, gct, bias_row, bias_col, zero, reverse=True, finalize=(hcf, hn))
    yx_pre, _ = _mlstm_scan(px, gx, gxt, bias_row, bias_col, st_b, reverse=True, finalize=(hxf, hn))

    outs = []
    for (xs, row), y_pre, live in zip(streams, (yx_pre, yc_pre), (True, ctx_out)):
        if live:
            outs.append(_fused_matmul(y_pre, w_out, n_out=d, epilogue="residual", res=xs, mods=mods,
                                      row=row, name="mlstm_out_proj"))
        else:
            outs.append(xs)
    return outs


def _conv_layer(streams, mods, norm, b_w_in, b_w_dw, ln, b_w_out, j, live_flags, seq_lens):
    d = b_w_in.shape[1]
    w_in = b_w_in[j].astype(BF16)
    w_out = b_w_out[j].astype(BF16)
    outs = []
    for (xs, row), live, seq_len in zip(streams, live_flags, seq_lens):
        if not live:
            outs.append(xs)
            continue
        hglu = _fused_matmul(xs, w_in, n_out=d, prologue="mod", norm=norm, mods=mods, row=row,
                             epilogue="glu", w2_col_offset=d, name="conv_in_glu")
        hc = _dwconv(hglu, b_w_dw, j, seq_len=seq_len)
        outs.append(_fused_matmul(hc, w_out, n_out=d, prologue="ln_silu", ln=(ln, j),
                                  epilogue="residual", res=xs, mods=mods, row=row, pro_rows=64,
                                  name="conv_out_proj"))
    return outs


def kernel(x, c, ctx, c_ctx, ada_down, ada_up, ada_bias, norm_mix, norm_ffn, a_w_in, a_gate_bias, a_head_norm, a_w_out, b_w_in, b_w_dw, b_ln_g, b_ln_b, b_w_out, w_router, router_bias, e_w_gate, e_w_up, e_w_down, s_w_gate, s_w_up, s_w_down, final_norm):
    bsz, t, d = x.shape
    assert bsz == 1 and c.shape[0] == 1 and ctx.shape[0] == 1
    depth = ada_down.shape[0]
    n_experts = w_router.shape[-1]
    t_ctx = ctx.shape[1]

    cond = jnp.zeros((SUBLANES, d), F32).at[0].set(c[0]).at[1].set(c_ctx)
    mods_all = _ada_all(cond, ada_down, ada_up, ada_bias)
    norm_mix3 = norm_mix.reshape(depth, 1, d)
    norm_ffn3 = norm_ffn.reshape(depth, 1, d)
    head_norm3 = a_head_norm.reshape(a_head_norm.shape[0], 1, -1)
    ln3 = (b_ln_g.reshape(-1, 1, d), b_ln_b.reshape(-1, 1, d))
    fn = final_norm.reshape(1, d)
    e_gate_t, e_up_t = jnp.swapaxes(e_w_gate, 2, 3), jnp.swapaxes(e_w_up, 2, 3)
    s_gate_t, s_up_t = jnp.swapaxes(s_w_gate, 1, 2), jnp.swapaxes(s_w_up, 1, 2)

    xs, cs = x[0], ctx[0]
    for i in range(depth):
        kind, j = i % 2, i // 2
        ctx_live = any(l % 2 == 0 for l in range(i + 1, depth))
        mods = (mods_all, i)
        norm = (norm_mix3, i)
        streams = [(xs, 0), (cs, 1)]
        if kind == 0:
            xs, cs = _mlstm_layer(streams, mods, norm, a_w_in, j, a_gate_bias[j], head_norm3, a_w_out, ctx_live)
        else:
            xs, cs = _conv_layer(streams, mods, norm, b_w_in, b_w_dw, ln3, b_w_out, j,
                                 (True, ctx_live), (GRID_W, t_ctx))

        wr = jnp.pad(w_router[i].T, ((0, LANES - n_experts), (0, 0)))
        rb = _pad_cols(router_bias[i].reshape(1, n_experts), LANES)
        last = i == depth - 1
        new = []
        for (s, row), live in zip(((xs, 0), (cs, 1)), (True, ctx_live)):
            if live:
                s = _moe_layer(s, i, row, mods_all, norm_ffn3, wr, rb, e_gate_t, e_up_t, e_w_down,
                               s_gate_t, s_up_t, s_w_down, fn if (last and row == 0) else None)
            new.append(s)
        xs, cs = new
    return xs[None]
```
